```python
import math
import jax, jax.numpy as jnp
from jax import lax
import numpy as np

D_MODEL = 2048
BATCH = 4
SEQ = 2048
DEPTH = 2
DEC_BATCH = 8
DEC_SEQ = 8
PAST_LEN = 16384
PAGE_SIZE = 128

D_MIX = D_MODEL
D_LRU = D_MIX // 4
LRU_BLOCKS = 8
LRU_BW = D_LRU // LRU_BLOCKS
LRU_C = 8.0
CONV_W = 4
D_SB = D_MIX // 4
SB_HEADS = 8
SB_HEAD_DIM = D_SB // SB_HEADS
SB_BLOCK = 128
SB_BIAS_HI = -5.0
SB_BIAS_LO = -10.0
D_GDN = D_MIX // 2
GDN_HEADS = 8
GDN_HEAD_DIM = D_GDN // GDN_HEADS
GDN_CHUNK = 64
D_FF = ((8 * D_MODEL // 3) + 255) // 256 * 256
FFN_CONV_W = 3
N_MOD = 6
EPS = 1e-6
SPLIT_SIZES = (D_LRU, D_LRU, D_SB, D_SB, D_SB, 3 * D_GDN, D_GDN, GDN_HEADS, GDN_HEADS)
D_IN = 2 * D_LRU + 3 * D_SB + 4 * D_GDN + 2 * GDN_HEADS

kernel_name = 'hymba_lru_stickbreak_gdn_convffn_step'

F32 = jnp.float32


def rmsnorm(x, g):
    xf = x.astype(F32)
    y = xf * lax.rsqrt(jnp.mean(xf * xf, axis=-1, keepdims=True) + EPS)
    return (y * g.astype(F32)).astype(x.dtype)


def l2norm(x):
    xf = x.astype(F32)
    return xf * lax.rsqrt(jnp.sum(xf * xf, axis=-1, keepdims=True) + EPS)


def split_columns(a, sizes):
    out, start = [], 0
    for s in sizes:
        out.append(a[..., start:start + s])
        start += s
    return out


def causal_dwconv(x, buf, w, b=None):
    width = w.shape[0]
    t = x.shape[1]
    xp = jnp.concatenate([buf.astype(x.dtype), x], axis=1)
    y = xp[:, 0:t] * w[0]
    for i in range(1, width):
        y = y + xp[:, i:i + t] * w[i]
    if b is not None:
        y = y + b
    return y, xp[:, t:]


def _lin_combine(left, right):
    a_l, u_l = left
    a_r, u_r = right
    return a_l * a_r, a_r * u_l + u_r


def rglru(x, h0, w_r, b_r, w_i, b_i, lam):
    bsz, t, _ = x.shape
    xb = x.reshape(bsz, t, LRU_BLOCKS, LRU_BW)
    r = jax.nn.sigmoid((jnp.einsum('btnc,ncd->btnd', xb, w_r).reshape(bsz, t, D_LRU) + b_r).astype(F32))
    i = jax.nn.sigmoid((jnp.einsum('btnc,ncd->btnd', xb, w_i).reshape(bsz, t, D_LRU) + b_i).astype(F32))
    log_a = LRU_C * r * jax.nn.log_sigmoid(lam.astype(F32))
    a = jnp.exp(log_a)
    u = jnp.sqrt(-jnp.expm1(2.0 * log_a)) * (i * x.astype(F32))
    a_cum, u_cum = lax.associative_scan(_lin_combine, (a, u), axis=1)
    h = a_cum * h0.astype(F32)[:, None, :] + u_cum
    return h, h[:, -1]


def stick_breaking(q, k, v, q_pos, k_pos, bias):
    bsz, tq, nh, dh = q.shape
    blk = min(SB_BLOCK, tq)
    nb = -(-tq // blk)
    pad = nb * blk - tq
    q = jnp.pad(q, ((0, 0), (0, pad), (0, 0), (0, 0)))
    q_pos = jnp.pad(q_pos, (0, pad), mode='edge')
    qb = q.reshape(bsz, nb, blk, nh, dh).transpose(1, 0, 2, 3, 4)
    pb = q_pos.reshape(nb, blk)
    scale = dh ** -0.5
    bias_f = bias.astype(F32)[None, :, None, None]

    def one_block(args):
        qi, pi = args
        z = jnp.einsum('bqhd,bkhd->bhqk', qi, k).astype(F32) * scale + bias_f
        causal = k_pos[None, :] < pi[:, None]
        log_keep = jnp.where(causal, jax.nn.log_sigmoid(-z), 0.0)
        shifted = jnp.concatenate([log_keep[..., 1:], jnp.zeros_like(log_keep[..., :1])], axis=-1)
        after = lax.cumsum(shifted, axis=3, reverse=True)
        w = jnp.where(causal, jnp.exp(jax.nn.log_sigmoid(z) + after), 0.0)
        return jnp.einsum('bhqk,bkhd->bqhd', w.astype(v.dtype), v)

    o = lax.map(one_block, (qb, pb))
    return o.transpose(1, 0, 2, 3, 4).reshape(bsz, nb * blk, nh, dh)[:, :tq]


def gated_delta(q, k, v, g, beta, s0):
    bsz, t, nh, dk = q.shape
    dv = v.shape[-1]
    c = GDN_CHUNK
    n = -(-t // c)
    pad = n * c - t

    def chunk4(a):
        a = jnp.pad(a, ((0, 0), (0, pad), (0, 0), (0, 0)))
        return a.reshape(bsz, n, c, nh, a.shape[-1]).transpose(1, 0, 3, 2, 4)

    def chunk3(a):
        a = jnp.pad(a, ((0, 0), (0, pad), (0, 0)))
        return a.reshape(bsz, n, c, nh).transpose(1, 0, 3, 2)

    q = chunk4(q) * dk ** -0.5
    k = chunk4(k)
    v = chunk4(v)
    g = lax.cumsum(chunk3(g), axis=3)
    beta = chunk3(beta)
    idx = jnp.arange(c)
    incl = idx[:, None] >= idx[None, :]
    strict = idx[:, None] > idx[None, :]
    decay = jnp.exp(jnp.where(incl, g[..., :, None] - g[..., None, :], -jnp.inf))
    kb = k * beta[..., None]
    lmat = jnp.where(strict, jnp.einsum('nbhcd,nbhed->nbhce', kb, k) * decay, 0.0)
    eye = jnp.eye(c, dtype=F32)
    rhs = jnp.concatenate([v * beta[..., None], kb * jnp.exp(g)[..., None]], axis=-1)
    sol = lax.linalg.triangular_solve(eye + lmat, rhs, left_side=True, lower=True, unit_diagonal=True)
    u, w = sol[..., :dv], sol[..., dv:]
    qk = jnp.einsum('nbhcd,nbhed->nbhce', q, k) * decay

    def step(s, inp):
        q_i, k_i, u_i, w_i, g_i, qk_i = inp
        v_new = u_i - jnp.einsum('bhcd,bhde->bhce', w_i, s)
        o = (jnp.einsum('bhcd,bhde->bhce', q_i * jnp.exp(g_i)[..., None], s)
             + jnp.einsum('bhce,bhef->bhcf', qk_i, v_new))
        g_last = g_i[..., -1]
        s = (s * jnp.exp(g_last)[..., None, None]
             + jnp.einsum('bhcd,bhce->bhde', k_i * jnp.exp(g_last[..., None] - g_i)[..., None], v_new))
        return s, o

    s, o = lax.scan(step, s0.astype(F32), (q, k, u, w, g, qk))
    o = o.transpose(1, 0, 3, 2, 4).reshape(bsz, n * c, nh, dv)[:, :t]
    return o, s


def trunk_layer(x, c, k_past, v_past, lru_buf, lru_h, gdn_buf, gdn_s, ffn_buf, p):
    bsz, t, _ = x.shape
    past = k_past.shape[1]
    dt = x.dtype
    mod = (jax.nn.silu(c) @ p['w_ada'] + p['b_ada']).reshape(bsz, N_MOD, 1, D_MODEL)
    shift_m, scale_m, gate_m, shift_f, scale_f, gate_f = (mod[:, j] for j in range(N_MOD))

    h = rmsnorm(x, p['g_pre_mix']) * (1.0 + scale_m) + shift_m
    proj = h @ p['w_in']
    a_x, a_g, b_q, b_k, b_v, c_qkv, c_z, c_b, c_a = split_columns(proj, SPLIT_SIZES)

    a_xc, lru_buf_new = causal_dwconv(a_x, lru_buf, p['conv_lru_w'], p['conv_lru_b'])
    a_h, h_last = rglru(a_xc, lru_h, p['w_lru_r'], p['b_lru_r'], p['w_lru_i'], p['b_lru_i'], p['lru_lambda'])
    y_a = rmsnorm((a_h * jax.nn.gelu(a_g.astype(F32))).astype(dt), p['g_grp_lru'])

    q = b_q.reshape(bsz, t, SB_HEADS, SB_HEAD_DIM)
    k_new = b_k.reshape(bsz, t, SB_HEADS, SB_HEAD_DIM)
    v_new = b_v.reshape(bsz, t, SB_HEADS, SB_HEAD_DIM)
    k_all = jnp.concatenate([k_past.astype(dt), k_new], axis=1)
    v_all = jnp.concatenate([v_past.astype(dt), v_new], axis=1)
    pos_k = jnp.arange(past + t, dtype=jnp.int32)
    pos_q = past + jnp.arange(t, dtype=jnp.int32)
    y_b = stick_breaking(q, k_all, v_all, pos_q, pos_k, p['sb_bias']).reshape(bsz, t, D_SB)
    y_b = rmsnorm(y_b, p['g_grp_sb'])

    c_qkv_c, gdn_buf_new = causal_dwconv(c_qkv, gdn_buf, p['conv_gdn_w'])
    c_qkv_c = jax.nn.silu(c_qkv_c)
    cq, ck, cv = jnp.split(c_qkv_c, 3, axis=-1)
    heads = lambda a: a.reshape(bsz, t, GDN_HEADS, GDN_HEAD_DIM)
    beta = jax.nn.sigmoid(c_b.astype(F32))
    g = -jnp.exp(p['gdn_a_log'].astype(F32)) * jax.nn.softplus(c_a.astype(F32) + p['gdn_dt_bias'].astype(F32))
    o_c, s_new = gated_delta(l2norm(heads(cq)), l2norm(heads(ck)), heads(cv).astype(F32), g, beta, gdn_s)
    o_c = rmsnorm(o_c, p['g_gdn_norm']) * jax.nn.silu(heads(c_z).astype(F32))
    y_c = o_c.astype(dt).reshape(bsz, t, D_GDN)

    mix = jnp.concatenate([y_a, y_b, y_c], axis=-1) @ p['w_out']
    x = x + gate_m * rmsnorm(mix, p['g_post_mix'])

    h = rmsnorm(x, p['g_pre_ffn']) * (1.0 + scale_f) + shift_f
    up = h @ p['w_ffn_up']
    gate, val = up[..., :D_FF], up[..., D_FF:]
    gate_c, ffn_buf_new = causal_dwconv(gate, ffn_buf, p['conv_ffn_w'], p['conv_ffn_b'])
    f = (jax.nn.silu(gate_c) * val) @ p['w_ffn_down']
    x = x + gate_f * rmsnorm(f, p['g_post_ffn'])
    return (x, k_new, v_new, lru_buf_new, h_last.astype(lru_h.dtype), gdn_buf_new,
            s_new.astype(gdn_s.dtype), ffn_buf_new)


def setup_inputs(seed: int = 0) -> dict:
    key = jax.random.key(seed)
    ks = iter(jax.random.split(key, 64))

    def nrm(shape, scale):
        return scale * jax.random.normal(next(ks), shape, F32)

    n_pages = PAST_LEN // PAGE_SIZE
    n_pool = (5 * DEC_BATCH * n_pages + 3) // 4
    page_table = jax.random.permutation(next(ks), n_pool)[: DEC_BATCH * n_pages].reshape(DEC_BATCH, n_pages).astype(jnp.int32)
    a_c = jax.random.uniform(next(ks), (DEPTH, D_LRU), F32, minval=0.9, maxval=0.999)
    a_base = a_c ** (1.0 / LRU_C)
    lru_lambda = jnp.log(a_base) - jnp.log1p(-a_base)
    gdn_a_log = jnp.log(jax.random.uniform(next(ks), (DEPTH, GDN_HEADS), F32, minval=1.0, maxval=16.0))
    dt0 = jnp.exp(jax.random.uniform(next(ks), (DEPTH, GDN_HEADS), F32, minval=math.log(1e-3), maxval=math.log(1e-1)))
    gdn_dt_bias = dt0 + jnp.log(-jnp.expm1(-dt0))
    sb_bias = jnp.linspace(SB_BIAS_HI, SB_BIAS_LO, SB_HEADS, dtype=F32)[None, :] + nrm((DEPTH, SB_HEADS), 0.1)
    gain = lambda shape: 1.0 + nrm(shape, 0.02)
    return {
        'x_prompt': nrm((BATCH, SEQ, D_MODEL), 1.0),
        'x_sample': nrm((DEC_BATCH, DEC_SEQ, D_MODEL), 1.0),
        'c_prompt': nrm((BATCH, D_MODEL), 1.0),
        'c_sample': nrm((DEC_BATCH, D_MODEL), 1.0),
        'cache_sb_k': nrm((DEPTH, n_pool, PAGE_SIZE, SB_HEADS, SB_HEAD_DIM), 1.0),
        'cache_sb_v': nrm((DEPTH, n_pool, PAGE_SIZE, SB_HEADS, SB_HEAD_DIM), 1.0),
        'page_table': page_table,
        'state_lru_conv': nrm((DEPTH, DEC_BATCH, CONV_W - 1, D_LRU), 1.0),
        'state_lru_h': nrm((DEPTH, DEC_BATCH, D_LRU), 0.5),
        'state_gdn_conv': nrm((DEPTH, DEC_BATCH, CONV_W - 1, 3 * D_GDN), 1.0),
        'state_gdn': nrm((DEPTH, DEC_BATCH, GDN_HEADS, GDN_HEAD_DIM, GDN_HEAD_DIM), 0.1),
        'state_ffn_conv': nrm((DEPTH, DEC_BATCH, FFN_CONV_W - 1, D_FF), 1.0),
        'w_ada': nrm((DEPTH, D_MODEL, N_MOD * D_MODEL), 0.5 * D_MODEL ** -0.5),
        'b_ada': nrm((DEPTH, N_MOD * D_MODEL), 0.01),
        'g_pre_mix': gain((DEPTH, D_MODEL)),
        'g_post_mix': gain((DEPTH, D_MODEL)),
        'g_pre_ffn': gain((DEPTH, D_MODEL)),
        'g_post_ffn': gain((DEPTH, D_MODEL)),
        'w_in': nrm((DEPTH, D_MODEL, D_IN), D_MODEL ** -0.5),
        'conv_lru_w': nrm((DEPTH, CONV_W, D_LRU), CONV_W ** -0.5),
        'conv_lru_b': nrm((DEPTH, D_LRU), 0.01),
        'w_lru_r': nrm((DEPTH, LRU_BLOCKS, LRU_BW, LRU_BW), LRU_BW ** -0.5),
        'b_lru_r': nrm((DEPTH, D_LRU), 0.01),
        'w_lru_i': nrm((DEPTH, LRU_BLOCKS, LRU_BW, LRU_BW), LRU_BW ** -0.5),
        'b_lru_i': nrm((DEPTH, D_LRU), 0.01),
        'lru_lambda': lru_lambda,
        'g_grp_lru': gain((DEPTH, D_LRU)),
        'g_grp_sb': gain((DEPTH, D_SB)),
        'sb_bias': sb_bias,
        'conv_gdn_w': nrm((DEPTH, CONV_W, 3 * D_GDN), CONV_W ** -0.5),
        'gdn_a_log': gdn_a_log,
        'gdn_dt_bias': gdn_dt_bias,
        'g_gdn_norm': gain((DEPTH, GDN_HEAD_DIM)),
        'w_out': nrm((DEPTH, D_MIX, D_MODEL), D_MIX ** -0.5),
        'w_ffn_up': nrm((DEPTH, D_MODEL, 2 * D_FF), D_MODEL ** -0.5),
        'conv_ffn_w': nrm((DEPTH, FFN_CONV_W, D_FF), FFN_CONV_W ** -0.5),
        'conv_ffn_b': nrm((DEPTH, D_FF), 0.01),
        'w_ffn_down': nrm((DEPTH, D_FF, D_MODEL), D_FF ** -0.5),
    }


def reference(x_prompt, x_sample, c_prompt, c_sample, cache_sb_k, cache_sb_v, page_table,
              state_lru_conv, state_lru_h, state_gdn_conv, state_gdn, state_ffn_conv,
              w_ada, b_ada, g_pre_mix, g_post_mix, g_pre_ffn, g_post_ffn, w_in,
              conv_lru_w, conv_lru_b, w_lru_r, b_lru_r, w_lru_i, b_lru_i, lru_lambda,
              g_grp_lru, g_grp_sb, sb_bias, conv_gdn_w, gdn_a_log, gdn_dt_bias, g_gdn_norm, w_out,
              w_ffn_up, conv_ffn_w, conv_ffn_b, w_ffn_down):
    bp = x_prompt.shape[0]
    bs = x_sample.shape[0]
    n_pages = page_table.shape[1]
    dt = x_prompt.dtype
    names = ('kp', 'vp', 'ks', 'vs', 'lcp', 'lcs', 'lhp', 'lhs', 'gcp', 'gcs', 'gsp', 'gss', 'fcp', 'fcs')
    new = {nm: [] for nm in names}
    xp, xs = x_prompt, x_sample
    for l in range(DEPTH):
        prm = {
            'w_ada': w_ada[l], 'b_ada': b_ada[l], 'g_pre_mix': g_pre_mix[l], 'g_post_mix': g_post_mix[l],
            'g_pre_ffn': g_pre_ffn[l], 'g_post_ffn': g_post_ffn[l], 'w_in': w_in[l],
            'conv_lru_w': conv_lru_w[l], 'conv_lru_b': conv_lru_b[l], 'w_lru_r': w_lru_r[l], 'b_lru_r': b_lru_r[l],
            'w_lru_i': w_lru_i[l], 'b_lru_i': b_lru_i[l], 'lru_lambda': lru_lambda[l],
            'g_grp_lru': g_grp_lru[l], 'g_grp_sb': g_grp_sb[l], 'sb_bias': sb_bias[l], 'conv_gdn_w': conv_gdn_w[l],
            'gdn_a_log': gdn_a_log[l], 'gdn_dt_bias': gdn_dt_bias[l], 'g_gdn_norm': g_gdn_norm[l],
            'w_out': w_out[l], 'w_ffn_up': w_ffn_up[l], 'conv_ffn_w': conv_ffn_w[l], 'conv_ffn_b': conv_ffn_b[l],
            'w_ffn_down': w_ffn_down[l],
        }
        kv0 = jnp.zeros((bp, 0, SB_HEADS, SB_HEAD_DIM), dt)
        xp, kp, vp, lcp, lhp, gcp, gsp, fcp = trunk_layer(
            xp, c_prompt, kv0, kv0,
            jnp.zeros((bp, CONV_W - 1, D_LRU), dt), jnp.zeros((bp, D_LRU), dt),
            jnp.zeros((bp, CONV_W - 1, 3 * D_GDN), dt),
            jnp.zeros((bp, GDN_HEADS, GDN_HEAD_DIM, GDN_HEAD_DIM), dt),
            jnp.zeros((bp, FFN_CONV_W - 1, D_FF), dt), prm)
        k_past = cache_sb_k[l][page_table].reshape(bs, n_pages * PAGE_SIZE, SB_HEADS, SB_HEAD_DIM)
        v_past = cache_sb_v[l][page_table].reshape(bs, n_pages * PAGE_SIZE, SB_HEADS, SB_HEAD_DIM)
        xs, ks_, vs_, lcs, lhs, gcs, gss, fcs = trunk_layer(
            xs, c_sample, k_past, v_past, state_lru_conv[l], state_lru_h[l],
            state_gdn_conv[l], state_gdn[l], state_ffn_conv[l], prm)
        for nm, arr in (('kp', kp), ('vp', vp), ('ks', ks_), ('vs', vs_), ('lcp', lcp), ('lcs', lcs),
                        ('lhp', lhp), ('lhs', lhs), ('gcp', gcp), ('gcs', gcs), ('gsp', gsp), ('gss', gss),
                        ('fcp', fcp), ('fcs', fcs)):
            new[nm].append(arr)
    return (xp, xs,
            jnp.stack(new['kp']), jnp.stack(new['vp']), jnp.stack(new['ks']), jnp.stack(new['vs']),
            jnp.stack(new['lcp']), jnp.stack(new['lcs']), jnp.stack(new['lhp']), jnp.stack(new['lhs']),
            jnp.stack(new['gcp']), jnp.stack(new['gcs']), jnp.stack(new['gsp']), jnp.stack(new['gss']),
            jnp.stack(new['fcp']), jnp.stack(new['fcs']))
```

```python
import functools
import math

import jax
import jax.numpy as jnp
from jax import lax
from jax.experimental import pallas as pl
from jax.experimental.pallas import tpu as pltpu

F32 = jnp.float32
BF16 = jnp.bfloat16

D_MODEL = 2048
D_LRU = 512
LRU_BLOCKS = 8
LRU_C = 8.0
CONV_W = 4
D_SB = 512
SB_HEADS = 8
SB_HEAD_DIM = 64
D_GDN = 1024
GDN_HEADS = 8
GDN_HEAD_DIM = 128
D_FF = 5632
FFN_CONV_W = 3
N_MOD = 6
EPS = 1e-6
PAGE_SIZE = 128
D_MAIN = 2 * D_LRU + 3 * D_SB + 4 * D_GDN
D_TAIL = 2 * GDN_HEADS
GDN_COL0 = (2 * D_LRU + 3 * D_SB) // 128

VMEM_LIMIT = 56 * 1024 * 1024
MOD_ROWS = 16


def _cparams(sem):
    return pltpu.CompilerParams(dimension_semantics=sem, vmem_limit_bytes=VMEM_LIMIT)


def _dot(a, b):
    return jnp.dot(a, b, preferred_element_type=F32)


def _dot_nt(a, b):
    return lax.dot_general(a, b, (((1,), (1,)), ((), ())), preferred_element_type=F32)


def _mm1(a, b):
    return _dot(a.astype(BF16), b.astype(BF16))


def _split(a):
    hi = a.astype(BF16)
    lo = (a - hi.astype(F32)).astype(BF16)
    return hi, lo


def _mm3(a, b):
    a_hi, a_lo = _split(a)
    b_hi, b_lo = _split(b)
    return _dot(a_hi, b_hi) + (_dot(a_hi, b_lo) + _dot(a_lo, b_hi))


def _mm_exact_rhs(a, b_bf16):
    a_hi, a_lo = _split(a)
    return _dot(a_hi, b_bf16) + _dot(a_lo, b_bf16)


def _softplus(z):
    return jnp.maximum(z, 0.0) + jnp.log1p(jnp.exp(-jnp.abs(z)))


def _sigmoid(z):
    return jax.nn.sigmoid(z)


def _rms(x, g):
    return x * lax.rsqrt(jnp.mean(x * x, axis=-1, keepdims=True) + EPS) * g


def _ada_kernel(c_ref, w_ref, b_ref, o_ref):
    c = c_ref[...]
    s = c * _sigmoid(c)
    o_ref[0] = _dot(s.astype(BF16), w_ref[0].astype(BF16)) + b_ref[0]


def _ada(c_all, w_ada, b_ada):
    depth, d, n = w_ada.shape
    tn = 1024
    return pl.pallas_call(
        _ada_kernel,
        grid=(depth, n // tn),
        in_specs=[pl.BlockSpec((MOD_ROWS, d), lambda l, j: (0, 0)),
                  pl.BlockSpec((1, d, tn), lambda l, j: (l, 0, j)),
                  pl.BlockSpec((1, 1, tn), lambda l, j: (l, 0, j))],
        out_specs=pl.BlockSpec((1, MOD_ROWS, tn), lambda l, j: (l, 0, j)),
        out_shape=jax.ShapeDtypeStruct((depth, MOD_ROWS, n), F32),
        compiler_params=_cparams(("arbitrary", "arbitrary")),
        name="ada",
    )(c_all, w_ada, b_ada.reshape(depth, 1, n))


def _in_proj_kernel(x_ref, g_ref, sh_ref, sc_ref, w_ref, wt_ref, wtt_ref, o_ref, ot_ref, ott_ref, h_scr):
    j = pl.program_id(1)

    @pl.when(j == 0)
    def _():
        h = _rms(x_ref[...], g_ref[...]) * (1.0 + sc_ref[0, 0]) + sh_ref[0, 0]
        hb = h.astype(BF16)
        h_scr[...] = hb
        ot_ref[...] = _dot(hb, wt_ref[...])
        ott_ref[...] = _dot_nt(wtt_ref[...], hb)

    o_ref[...] = _dot(h_scr[...], w_ref[...])


def _in_proj(x, g, mod, w_main, w_tail, w_tail_t, *, tm, tiles_per_group):
    m, d = x.shape
    n = w_main.shape[1]
    tn = 512
    mrows = mod.shape[2]
    mod_spec = lambda k: pl.BlockSpec((1, 1, mrows, d), lambda i, j: (k, i // tiles_per_group, 0, 0))
    return pl.pallas_call(
        _in_proj_kernel,
        grid=(m // tm, n // tn),
        in_specs=[pl.BlockSpec((tm, d), lambda i, j: (i, 0)),
                  pl.BlockSpec((1, d), lambda i, j: (0, 0)),
                  mod_spec(0), mod_spec(1),
                  pl.BlockSpec((d, tn), lambda i, j: (0, j)),
                  pl.BlockSpec((d, 128), lambda i, j: (0, 0)),
                  pl.BlockSpec((D_TAIL, d), lambda i, j: (0, 0))],
        out_specs=[pl.BlockSpec((tm, tn), lambda i, j: (i, j)),
                   pl.BlockSpec((tm, 128), lambda i, j: (i, 0)),
                   pl.BlockSpec((D_TAIL, tm), lambda i, j: (0, i))],
        out_shape=[jax.ShapeDtypeStruct((m, n), F32),
                   jax.ShapeDtypeStruct((m, 128), F32),
                   jax.ShapeDtypeStruct((D_TAIL, m), F32)],
        scratch_shapes=[pltpu.VMEM((tm, d), BF16)],
        compiler_params=_cparams(("arbitrary", "arbitrary")),
        name="in_proj",
    )(x, g, mod, mod, w_main, w_tail, w_tail_t)


def _lru_kernel(ax_ref, ag_ref, buf_ref, h0_ref, cw_ref, cb_ref, wr_ref, br_ref, wi_ref, bi_ref, lam_ref, gg_ref,
                y_ref, hl_ref, xp_scr, h_scr, *, rows):
    t = pl.program_id(1)

    @pl.when(t == 0)
    def _():
        xp_scr[5:8, :] = buf_ref[0]
        h_scr[...] = h0_ref[0]

    x = ax_ref[...]
    xp_scr[8:8 + rows, :] = x
    cw = cw_ref[...]
    xc = (xp_scr[5:5 + rows, :] * cw[0:1] + xp_scr[6:6 + rows, :] * cw[1:2]
          + xp_scr[7:7 + rows, :] * cw[2:3] + x * cw[3:4]) + cb_ref[...]
    xp_scr[5:8, :] = xp_scr[5 + rows:8 + rows, :]

    xb = xc.astype(BF16)
    r = _sigmoid(_dot(xb, wr_ref[...]) + br_ref[...])
    ig = _sigmoid(_dot(xb, wi_ref[...]) + bi_ref[...])
    log_a = (LRU_C * r) * (-_softplus(-lam_ref[...]))
    a = jnp.exp(log_a)
    u = jnp.sqrt(1.0 - a * a) * (ig * xc)

    ridx = lax.broadcasted_iota(jnp.int32, a.shape, 0)
    s = 1
    while s < rows:
        a_sh = pltpu.roll(a, s, 0)
        u_sh = pltpu.roll(u, s, 0)
        m = ridx >= s
        u = jnp.where(m, a * u_sh + u, u)
        a = jnp.where(m, a * a_sh, a)
        s *= 2
    h = a * h_scr[...] + u
    h_last = h[rows - 1:rows, :]
    h_scr[...] = h_last
    hl_ref[0] = h_last

    ag = ag_ref[...]
    gelu = 0.5 * ag * (1.0 + jnp.tanh(math.sqrt(2.0 / math.pi) * (ag + 0.044715 * (ag * ag * ag))))
    y_ref[...] = _rms(h * gelu, gg_ref[...])


def _lru(proj, buf, h0, cw, cb, wr, br, wi, bi, lam, gg, *, nb, t_len, rows):
    m = proj.shape[0]
    nt = t_len // rows
    row_map = lambda c: (lambda b, t: (b * nt + t, c))
    vec = lambda: pl.BlockSpec((1, D_LRU), lambda b, t: (0, 0))
    return pl.pallas_call(
        functools.partial(_lru_kernel, rows=rows),
        grid=(nb, nt),
        in_specs=[pl.BlockSpec((rows, D_LRU), row_map(0)),
                  pl.BlockSpec((rows, D_LRU), row_map(1)),
                  pl.BlockSpec((1, CONV_W - 1, D_LRU), lambda b, t: (b, 0, 0)),
                  pl.BlockSpec((1, 1, D_LRU), lambda b, t: (b, 0, 0)),
                  pl.BlockSpec((CONV_W, D_LRU), lambda b, t: (0, 0)),
                  vec(),
                  pl.BlockSpec((D_LRU, D_LRU), lambda b, t: (0, 0)), vec(),
                  pl.BlockSpec((D_LRU, D_LRU), lambda b, t: (0, 0)), vec(),
                  vec(), vec()],
        out_specs=[pl.BlockSpec((rows, D_LRU), lambda b, t: (b * nt + t, 0)),
                   pl.BlockSpec((1, 1, D_LRU), lambda b, t: (b, 0, 0))],
        out_shape=[jax.ShapeDtypeStruct((m, D_LRU), F32),
                   jax.ShapeDtypeStruct((nb, 1, D_LRU), F32)],
        scratch_shapes=[pltpu.VMEM((rows + 8, D_LRU), F32), pltpu.VMEM((1, D_LRU), F32)],
        compiler_params=_cparams(("arbitrary", "arbitrary")),
        name="lru",
    )(proj, proj, buf, h0, cw, cb, wr, br, wi, bi, lam, gg)


def _sb_block(z, mask, tri, carry):
    sp = _softplus(z)
    lk = -sp
    if mask is not None:
        lk = jnp.where(mask, lk, 0.0)
    after = _mm_exact_rhs(lk, tri) + carry
    w = jnp.exp((z - sp) + after)
    if mask is not None:
        w = jnp.where(mask, w, 0.0)
    return w, carry + jnp.sum(lk, axis=1, keepdims=True)


def _sb_prompt_kernel(bias_ref, q_ref, k_ref, v_ref, o_ref, *, tq):
    qi = pl.program_id(1)
    scale = SB_HEAD_DIM ** -0.5
    row = lax.broadcasted_iota(jnp.int32, (tq, tq), 0)
    col = lax.broadcasted_iota(jnp.int32, (tq, tq), 1)
    tri = jnp.where(row > col, 1.0, 0.0).astype(BF16)
    causal = col < row

    for h in range(SB_HEADS):
        sl = slice(h * SB_HEAD_DIM, (h + 1) * SB_HEAD_DIM)
        qh = (q_ref[:, sl] * scale).astype(BF16)
        bias = bias_ref[h]

        def block(kb, carry, acc, mask):
            start = pl.multiple_of(kb * tq, tq)
            kh = k_ref[pl.ds(start, tq), sl].astype(BF16)
            vh = v_ref[pl.ds(start, tq), sl].astype(BF16)
            z = _dot_nt(qh, kh) + bias
            w, carry = _sb_block(z, mask, tri, carry)
            return carry, acc + _dot(w.astype(BF16), vh)

        carry, acc = block(qi, jnp.zeros((tq, 1), F32), jnp.zeros((tq, SB_HEAD_DIM), F32), causal)

        def body(it, ca):
            return block(qi - 1 - it, ca[0], ca[1], None)

        carry, acc = lax.fori_loop(0, qi, body, (carry, acc))
        o_ref[:, sl] = acc


def _sb_prompt(proj, bias, *, nb, t_len, tq):
    m = proj.shape[0]
    nq = t_len // tq
    qcol, kcol, vcol = (2 * D_LRU) // D_SB, (2 * D_LRU) // D_SB + 1, (2 * D_LRU) // D_SB + 2
    return pl.pallas_call(
        functools.partial(_sb_prompt_kernel, tq=tq),
        grid=(nb, nq),
        in_specs=[pl.BlockSpec(memory_space=pltpu.SMEM),
                  pl.BlockSpec((tq, D_SB), lambda b, i: (b * nq + i, qcol)),
                  pl.BlockSpec((t_len, D_SB), lambda b, i: (b, kcol)),
                  pl.BlockSpec((t_len, D_SB), lambda b, i: (b, vcol))],
        out_specs=pl.BlockSpec((tq, D_SB), lambda b, i: (b * nq + i, 0)),
        out_shape=jax.ShapeDtypeStruct((m, D_SB), F32),
        compiler_params=_cparams(("arbitrary", "arbitrary")),
        name="sb_prompt",
    )(bias, proj, proj, proj)


def _sb_sample_kernel(pt_ref, q_ref, bias_ref, kn_ref, vn_ref, kc_ref, vc_ref, o_ref, acc_scr, car_scr, *, t_new):
    del pt_ref
    p = pl.program_id(1)
    rows = SB_HEADS * t_new
    scale = SB_HEAD_DIM ** -0.5
    q = q_ref[0] * scale
    q_rep = jnp.concatenate([q] * SB_HEADS, axis=0)
    rh = lax.broadcasted_iota(jnp.int32, (rows, D_SB), 0) >> int(math.log2(t_new))
    ch = lax.broadcasted_iota(jnp.int32, (rows, D_SB), 1) >> int(math.log2(SB_HEAD_DIM))
    own = rh == ch
    qbd = jnp.where(own, q_rep, 0.0).astype(BF16)
    trow = lax.broadcasted_iota(jnp.int32, (PAGE_SIZE, PAGE_SIZE), 0)
    tcol = lax.broadcasted_iota(jnp.int32, (PAGE_SIZE, PAGE_SIZE), 1)
    tri = jnp.where(trow > tcol, 1.0, 0.0).astype(BF16)

    def block(kblk, vblk, mask):
        z = _dot_nt(qbd, kblk.astype(BF16)) + bias_ref[...]
        w, carry = _sb_block(z, mask, tri, car_scr[...])
        car_scr[...] = carry
        acc_scr[...] += _dot(w.astype(BF16), vblk.astype(BF16))

    @pl.when(p == 0)
    def _():
        acc_scr[...] = jnp.zeros_like(acc_scr)
        car_scr[...] = jnp.zeros_like(car_scr)
        qpos = lax.broadcasted_iota(jnp.int32, (rows, PAGE_SIZE), 0) & (t_new - 1)
        kpos = lax.broadcasted_iota(jnp.int32, (rows, PAGE_SIZE), 1)
        block(kn_ref[0], vn_ref[0], kpos < qpos)

    block(kc_ref[...], vc_ref[...], None)

    @pl.when(p == pl.num_programs(1) - 1)
    def _():
        acc = jnp.where(own, acc_scr[...], 0.0)
        out = acc[0:t_new]
        for h in range(1, SB_HEADS):
            out = out + acc[h * t_new:(h + 1) * t_new]
        o_ref[0] = out


def _sb_sample(q, bias_rows, k_new, v_new, cache_k, cache_v, page_table, *, layer):
    nb, t_new, _ = q.shape
    n_pages = page_table.shape[1]
    rows = SB_HEADS * t_new
    cache_map = lambda b, p, pt: (layer, pt[b, n_pages - 1 - p], 0, 0)
    grid_spec = pltpu.PrefetchScalarGridSpec(
        num_scalar_prefetch=1,
        grid=(nb, n_pages),
        in_specs=[pl.BlockSpec((1, t_new, D_SB), lambda b, p, pt: (b, 0, 0)),
                  pl.BlockSpec((rows, PAGE_SIZE), lambda b, p, pt: (0, 0)),
                  pl.BlockSpec((1, PAGE_SIZE, D_SB), lambda b, p, pt: (b, 0, 0)),
                  pl.BlockSpec((1, PAGE_SIZE, D_SB), lambda b, p, pt: (b, 0, 0)),
                  pl.BlockSpec((None, None, PAGE_SIZE, D_SB), cache_map),
                  pl.BlockSpec((None, None, PAGE_SIZE, D_SB), cache_map)],
        out_specs=pl.BlockSpec((1, t_new, D_SB), lambda b, p, pt: (b, 0, 0)),
        scratch_shapes=[pltpu.VMEM((rows, D_SB), F32), pltpu.VMEM((rows, 1), F32)],
    )
    return pl.pallas_call(
        functools.partial(_sb_sample_kernel, t_new=t_new),
        grid_spec=grid_spec,
        out_shape=jax.ShapeDtypeStruct((nb, t_new, D_SB), F32),
        compiler_params=_cparams(("arbitrary", "arbitrary")),
        name="sb_sample",
    )(page_table, q, bias_rows, k_new, v_new, cache_k, cache_v)


def _unit_lower_inverse(lmat, c):
    ii = lax.broadcasted_iota(jnp.int32, (c, c), 0)
    jj = lax.broadcasted_iota(jnp.int32, (c, c), 1)
    eye = jnp.where(ii == jj, 1.0, 0.0)
    base = min(16, c)
    shift = int(math.log2(base))
    lb = jnp.where((ii >> shift) == (jj >> shift), lmat, 0.0)
    x = eye - lb
    pw = lb
    n = 2
    while n < base:
        pw = _mm3(pw, pw)
        x = x + _mm3(x, pw)
        n *= 2
    s = base
    while s < c:
        sh = int(math.log2(s))
        e = jnp.where(((ii >> (sh + 1)) == (jj >> (sh + 1))) & ((ii >> sh) != (jj >> sh)), lmat, 0.0)
        x = x - _mm3(_mm3(x, e), x)
        s *= 2
    return x


def _gdn_kernel(q_ref, k_ref, v_ref, z_ref, tail_ref, cwq_ref, cwk_ref, cwv_ref, bq_ref, bk_ref, bv_ref,
                alog_ref, dtb_ref, s0_ref, gn_ref, y_ref, s_ref, xq_scr, xk_scr, xv_scr,
                *, rows, chunk, valid_rows):
    h = pl.program_id(1)
    t = pl.program_id(2)
    dk = GDN_HEAD_DIM

    @pl.when(t == 0)
    def _():
        xq_scr[5:8, :] = bq_ref[0]
        xk_scr[5:8, :] = bk_ref[0]
        xv_scr[5:8, :] = bv_ref[0]
        s_ref[0, 0] = s0_ref[0, 0]

    def conv_silu(x_ref, scr, cw_ref):
        x = x_ref[...]
        scr[8:8 + rows, :] = x
        cw = cw_ref[...]
        y = (scr[5:5 + rows, :] * cw[0:1] + scr[6:6 + rows, :] * cw[1:2]
             + scr[7:7 + rows, :] * cw[2:3] + x * cw[3:4])
        scr[5:8, :] = scr[5 + rows:8 + rows, :]
        return y * _sigmoid(y)

    q = conv_silu(q_ref, xq_scr, cwq_ref)
    k = conv_silu(k_ref, xk_scr, cwk_ref)
    v = conv_silu(v_ref, xv_scr, cwv_ref)
    qn = q * lax.rsqrt(jnp.sum(q * q, axis=-1, keepdims=True) + EPS) * (dk ** -0.5)
    kn = k * lax.rsqrt(jnp.sum(k * k, axis=-1, keepdims=True) + EPS)

    tail = tail_ref[...]
    lane = lax.broadcasted_iota(jnp.int32, tail.shape, 1)
    b_col = jnp.sum(jnp.where(lane == h, tail, 0.0), axis=1, keepdims=True)
    a_col = jnp.sum(jnp.where(lane == GDN_HEADS + h, tail, 0.0), axis=1, keepdims=True)
    lane1 = lax.broadcasted_iota(jnp.int32, (1, 128), 1)
    a_log = jnp.sum(jnp.where(lane1 == h, alog_ref[...], 0.0), axis=1, keepdims=True)
    dt_b = jnp.sum(jnp.where(lane1 == h, dtb_ref[...], 0.0), axis=1, keepdims=True)
    beta = _sigmoid(b_col)
    g = -jnp.exp(a_log) * _softplus(a_col + dt_b)
    if valid_rows is not None:
        ok = lax.broadcasted_iota(jnp.int32, (rows, 1), 0) + t * rows < valid_rows
        beta = jnp.where(ok, beta, 0.0)
        g = jnp.where(ok, g, 0.0)

    ii = lax.broadcasted_iota(jnp.int32, (chunk, chunk), 0)
    jj = lax.broadcasted_iota(jnp.int32, (chunk, chunk), 1)
    incl = ii >= jj
    strict = ii > jj
    ltri = jnp.where(incl, 1.0, 0.0).astype(BF16)

    pre = []
    for c in range(rows // chunk):
        sl = slice(c * chunk, (c + 1) * chunk)
        g_c, beta_c, q_c, k_c, v_c = g[sl], beta[sl], qn[sl], kn[sl], v[sl]
        gcum = _mm_exact_rhs_lhs(ltri, jnp.broadcast_to(g_c, (chunk, dk)))
        gdiff = _mm_exact_rhs_lhs(ltri, jnp.where(strict, jnp.broadcast_to(g_c, (chunk, chunk)), 0.0))
        decay = jnp.where(incl, jnp.exp(gdiff), 0.0)
        kb = k_c * beta_c
        lmat = jnp.where(strict, _dot_nt(kb.astype(BF16), k_c.astype(BF16)) * decay, 0.0)
        tinv = _unit_lower_inverse(lmat, chunk)
        u = _mm1(tinv, v_c * beta_c)
        w = _mm1(tinv, kb * jnp.exp(gcum))
        qk = _dot_nt(q_c.astype(BF16), k_c.astype(BF16)) * decay
        g_last = gcum[chunk - 1:chunk, :]
        pre.append((u, w, qk, q_c * jnp.exp(gcum), k_c * jnp.exp(g_last - gcum), jnp.exp(g_last)))

    s = s_ref[0, 0]
    gn = gn_ref[...]
    for c, (u, w, qk, qg, kd, eg) in enumerate(pre):
        sl = slice(c * chunk, (c + 1) * chunk)
        sb = s.astype(BF16)
        v_new = u - _dot(w.astype(BF16), sb)
        o = _dot(qg.astype(BF16), sb) + _mm1(qk, v_new)
        s = s * eg + _dot(kd.T.astype(BF16), v_new.astype(BF16))
        zc = z_ref[sl, :]
        y_ref[sl, :] = _rms(o, gn) * (zc * _sigmoid(zc))
    s_ref[0, 0] = s


def _mm_exact_rhs_lhs(a_bf16, b):
    b_hi, b_lo = _split(b)
    return _dot(a_bf16, b_hi) + _dot(a_bf16, b_lo)


def _gdn(arr, tail, col0, cw, buf, a_log, dt_bias, s0, gn, *, nb, t_len, rows, chunk, valid_rows):
    m = arr.shape[0]
    nt = t_len // rows
    hd = GDN_HEAD_DIM
    col = lambda base: pl.BlockSpec((rows, hd), lambda b, h, t: (b * nt + t, col0 + base + h))
    cwspec = lambda base: pl.BlockSpec((CONV_W, hd), lambda b, h, t: (0, base + h))
    bufspec = lambda base: pl.BlockSpec((1, CONV_W - 1, hd), lambda b, h, t: (b, 0, base + h))
    vec = lambda: pl.BlockSpec((1, 128), lambda b, h, t: (0, 0))
    sspec = lambda: pl.BlockSpec((1, 1, hd, hd), lambda b, h, t: (b, h, 0, 0))
    nh = GDN_HEADS
    return pl.pallas_call(
        functools.partial(_gdn_kernel, rows=rows, chunk=chunk, valid_rows=valid_rows),
        grid=(nb, nh, nt),
        in_specs=[col(0), col(nh), col(2 * nh), col(3 * nh),
                  pl.BlockSpec((rows, 128), lambda b, h, t: (b * nt + t, 0)),
                  cwspec(0), cwspec(nh), cwspec(2 * nh),
                  bufspec(0), bufspec(nh), bufspec(2 * nh),
                  vec(), vec(), sspec(), vec()],
        out_specs=[pl.BlockSpec((rows, hd), lambda b, h, t: (b * nt + t, h)), sspec()],
        out_shape=[jax.ShapeDtypeStruct((m, D_GDN), F32),
                   jax.ShapeDtypeStruct((nb, nh, hd, hd), F32)],
        scratch_shapes=[pltpu.VMEM((rows + 8, hd), F32)] * 3,
        compiler_params=_cparams(("arbitrary", "arbitrary", "arbitrary")),
        name="gdn",
    )(arr, arr, arr, arr, tail, cw, cw, cw, buf, buf, buf, a_log, dt_bias, s0, gn)


def _out_proj_kernel(ya_ref, ob_ref, yc_ref, x_ref, gate_ref, gsb_ref, gpost_ref, w_ref, o_ref):
    yb = _rms(ob_ref[...], gsb_ref[...])
    mix = (_dot(ya_ref[...].astype(BF16), w_ref[0:D_LRU, :])
           + _dot(yb.astype(BF16), w_ref[D_LRU:D_LRU + D_SB, :])
           + _dot(yc_ref[...].astype(BF16), w_ref[D_LRU + D_SB:, :]))
    o_ref[...] = x_ref[...] + gate_ref[0, 0] * _rms(mix, gpost_ref[...])


def _out_proj(ya, ob, yc, x, mod, gsb, gpost, w_out, *, tm, tiles_per_group):
    m, d = x.shape
    mrows = mod.shape[2]
    return pl.pallas_call(
        _out_proj_kernel,
        grid=(m // tm,),
        in_specs=[pl.BlockSpec((tm, D_LRU), lambda i: (i, 0)),
                  pl.BlockSpec((tm, D_SB), lambda i: (i, 0)),
                  pl.BlockSpec((tm, D_GDN), lambda i: (i, 0)),
                  pl.BlockSpec((tm, d), lambda i: (i, 0)),
                  pl.BlockSpec((1, 1, mrows, d), lambda i: (2, i // tiles_per_group, 0, 0)),
                  pl.BlockSpec((1, D_SB), lambda i: (0, 0)),
                  pl.BlockSpec((1, d), lambda i: (0, 0)),
                  pl.BlockSpec((d, d), lambda i: (0, 0))],
        out_specs=pl.BlockSpec((tm, d), lambda i: (i, 0)),
        out_shape=jax.ShapeDtypeStruct((m, d), F32),
        compiler_params=_cparams(("arbitrary",)),
        name="out_proj",
    )(ya, ob, yc, x, mod, gsb, gpost, w_out)


def _ffn_kernel(x_ref, g_ref, sh_ref, sc_ref, gate_ref, gpost_ref, wg_ref, wv_ref, wd_ref, cw_ref, cb_ref, cin_ref,
                o_ref, cout_ref, h_scr, acc_scr, gbuf_scr, carry_scr, *, tm, rpt, tiles_per_group):
    i = pl.program_id(0)
    j = pl.program_id(1)
    pad = gbuf_scr.shape[0] - tm
    keep = (FFN_CONV_W - 1) * rpt

    @pl.when(j == 0)
    def _():
        h = _rms(x_ref[...], g_ref[...]) * (1.0 + sc_ref[0, 0]) + sh_ref[0, 0]
        h_scr[...] = h.astype(BF16)
        acc_scr[...] = jnp.zeros_like(acc_scr)

    first = (i % tiles_per_group) == 0

    @pl.when(first)
    def _():
        gbuf_scr[pad - keep:pad, :] = cin_ref[0]

    @pl.when(jnp.logical_not(first))
    def _():
        gbuf_scr[pad - keep:pad, :] = carry_scr[j]

    hb = h_scr[...]
    gate = _dot(hb, wg_ref[...])
    val = _dot(hb, wv_ref[...])
    gbuf_scr[pad:pad + tm, :] = gate
    cw = cw_ref[...]
    gc = (gbuf_scr[pad - 2 * rpt:pad - 2 * rpt + tm, :] * cw[0:1]
          + gbuf_scr[pad - rpt:pad - rpt + tm, :] * cw[1:2] + gate * cw[2:3]) + cb_ref[...]
    last = gbuf_scr[pad + tm - keep:pad + tm, :]
    carry_scr[j] = last
    cout_ref[0] = last
    f = (gc * _sigmoid(gc)) * val
    acc_scr[...] += _dot(f.astype(BF16), wd_ref[...])

    @pl.when(j == pl.num_programs(1) - 1)
    def _():
        o_ref[...] = x_ref[...] + gate_ref[0, 0] * _rms(acc_scr[...], gpost_ref[...])


def _ffn(x, g, mod, gpost, w_up, w_down, cw, cb, carry_in, *, tm, rpt, tiles_per_group, tf):
    m, d = x.shape
    nj = D_FF // tf
    mrows = mod.shape[2]
    keep = (FFN_CONV_W - 1) * rpt
    pad = -(-keep // 8) * 8
    mod_spec = lambda k: pl.BlockSpec((1, 1, mrows, d), lambda i, j: (k, i // tiles_per_group, 0, 0))
    return pl.pallas_call(
        functools.partial(_ffn_kernel, tm=tm, rpt=rpt, tiles_per_group=tiles_per_group),
        grid=(m // tm, nj),
        in_specs=[pl.BlockSpec((tm, d), lambda i, j: (i, 0)),
                  pl.BlockSpec((1, d), lambda i, j: (0, 0)),
                  mod_spec(3), mod_spec(4), mod_spec(5),
                  pl.BlockSpec((1, d), lambda i, j: (0, 0)),
                  pl.BlockSpec((d, tf), lambda i, j: (0, j)),
                  pl.BlockSpec((d, tf), lambda i, j: (0, nj + j)),
                  pl.BlockSpec((tf, d), lambda i, j: (j, 0)),
                  pl.BlockSpec((FFN_CONV_W, tf), lambda i, j: (0, j)),
                  pl.BlockSpec((1, tf), lambda i, j: (0, j)),
                  pl.BlockSpec((1, keep, tf), lambda i, j: (i // tiles_per_group, 0, j))],
        out_specs=[pl.BlockSpec((tm, d), lambda i, j: (i, 0)),
                   pl.BlockSpec((1, keep, tf), lambda i, j: (i, 0, j))],
        out_shape=[jax.ShapeDtypeStruct((m, d), F32),
                   jax.ShapeDtypeStruct((m // tm, keep, D_FF), F32)],
        scratch_shapes=[pltpu.VMEM((tm, d), BF16), pltpu.VMEM((tm, d), F32),
                        pltpu.VMEM((pad + tm, tf), F32), pltpu.VMEM((nj, keep, tf), F32)],
        compiler_params=_cparams(("arbitrary", "arbitrary")),
        name="ffn",
    )(x, g, mod, mod, mod, gpost, w_up, w_up, w_down, cw, cb, carry_in)


def _block_diag(w):
    nblk, bw, _ = w.shape
    eye = jnp.eye(nblk, dtype=w.dtype)
    return (eye[:, None, :, None] * w[:, :, None, :]).reshape(nblk * bw, nblk * bw)


def _prep_layer_weights(p):
    w_in = p['w_in']
    row = lambda a: a.reshape(1, -1)
    pad_lanes = lambda a: jnp.pad(a.reshape(1, -1), ((0, 0), (0, 128 - a.shape[-1])))
    return {
        'w_main': w_in[:, :D_MAIN].astype(BF16),
        'w_tail': jnp.pad(w_in[:, D_MAIN:], ((0, 0), (0, 128 - D_TAIL))).astype(BF16),
        'w_tail_t': w_in[:, D_MAIN:].T.astype(BF16),
        'g_pre_mix': row(p['g_pre_mix']), 'g_post_mix': row(p['g_post_mix']),
        'g_pre_ffn': row(p['g_pre_ffn']), 'g_post_ffn': row(p['g_post_ffn']),
        'conv_lru_w': p['conv_lru_w'], 'conv_lru_b': row(p['conv_lru_b']),
        'w_lru_r': _block_diag(p['w_lru_r']).astype(BF16), 'b_lru_r': row(p['b_lru_r']),
        'w_lru_i': _block_diag(p['w_lru_i']).astype(BF16), 'b_lru_i': row(p['b_lru_i']),
        'lru_lambda': row(p['lru_lambda']), 'g_grp_lru': row(p['g_grp_lru']), 'g_grp_sb': row(p['g_grp_sb']),
        'sb_bias': p['sb_bias'],
        'conv_gdn_w': p['conv_gdn_w'],
        'gdn_a_log': pad_lanes(p['gdn_a_log']), 'gdn_dt_bias': pad_lanes(p['gdn_dt_bias']),
        'g_gdn_norm': row(p['g_gdn_norm']),
        'w_out': p['w_out'].astype(BF16),
        'w_ffn_up': p['w_ffn_up'].astype(BF16), 'w_ffn_down': p['w_ffn_down'].astype(BF16),
        'conv_ffn_w': p['conv_ffn_w'], 'conv_ffn_b': row(p['conv_ffn_b']),
    }


def _layer_prompt(x, mod, w, *, nb, t_len):
    tm = 512
    tpg = t_len // tm
    proj, tail, _ = _in_proj(x, w['g_pre_mix'], mod, w['w_main'], w['w_tail'], w['w_tail_t'],
                             tm=tm, tiles_per_group=tpg)
    zeros = lambda *s: jnp.zeros(s, F32)
    ya, h_last = _lru(proj, zeros(nb, CONV_W - 1, D_LRU), zeros(nb, 1, D_LRU), w['conv_lru_w'], w['conv_lru_b'],
                      w['w_lru_r'], w['b_lru_r'], w['w_lru_i'], w['b_lru_i'], w['lru_lambda'], w['g_grp_lru'],
                      nb=nb, t_len=t_len, rows=256)
    ob = _sb_prompt(proj, w['sb_bias'], nb=nb, t_len=t_len, tq=256)
    yc, s_new = _gdn(proj, tail, GDN_COL0, w['conv_gdn_w'], zeros(nb, CONV_W - 1, 3 * D_GDN),
                     w['gdn_a_log'], w['gdn_dt_bias'], zeros(nb, GDN_HEADS, GDN_HEAD_DIM, GDN_HEAD_DIM),
                     w['g_gdn_norm'], nb=nb, t_len=t_len, rows=256, chunk=128, valid_rows=None)
    x1 = _out_proj(ya, ob, yc, x, mod, w['g_grp_sb'], w['g_post_mix'], w['w_out'], tm=256,
                   tiles_per_group=t_len // 256)
    x2, ffn_conv = _ffn(x1, w['g_pre_ffn'], mod, w['g_post_ffn'], w['w_ffn_up'], w['w_ffn_down'],
                        w['conv_ffn_w'], w['conv_ffn_b'], zeros(nb, FFN_CONV_W - 1, D_FF),
                        tm=tm, rpt=1, tiles_per_group=tpg, tf=512)
    ffn_conv = ffn_conv[tpg - 1::tpg]
    p3 = proj.reshape(nb, t_len, D_MAIN)
    k_new = p3[:, :, 3 * D_SB:4 * D_SB].reshape(nb, t_len, SB_HEADS, SB_HEAD_DIM)
    v_new = p3[:, :, 4 * D_SB:5 * D_SB].reshape(nb, t_len, SB_HEADS, SB_HEAD_DIM)
    lru_conv = p3[:, t_len - (CONV_W - 1):, 0:D_LRU]
    gdn_conv = p3[:, t_len - (CONV_W - 1):, GDN_COL0 * 128:GDN_COL0 * 128 + 3 * D_GDN]
    return x2, (k_new, v_new, lru_conv, h_last.reshape(nb, D_LRU), gdn_conv, s_new, ffn_conv)


def _layer_sample(x, mod_bm, mod_tm, w, layer, cache_k, cache_v, page_table, lru_buf, lru_h, gdn_buf, gdn_s,
                  ffn_buf, *, nb, t_len):
    m = nb * t_len
    proj, tail, _ = _in_proj(x, w['g_pre_mix'], mod_bm, w['w_main'], w['w_tail'], w['w_tail_t'],
                             tm=m, tiles_per_group=1)
    ya, h_last = _lru(proj, lru_buf, lru_h.reshape(nb, 1, D_LRU), w['conv_lru_w'], w['conv_lru_b'],
                      w['w_lru_r'], w['b_lru_r'], w['w_lru_i'], w['b_lru_i'], w['lru_lambda'], w['g_grp_lru'],
                      nb=nb, t_len=t_len, rows=t_len)
    p3 = proj.reshape(nb, t_len, D_MAIN)
    q = p3[:, :, 2 * D_SB:3 * D_SB]
    k_new = p3[:, :, 3 * D_SB:4 * D_SB]
    v_new = p3[:, :, 4 * D_SB:5 * D_SB]
    pad_rows = lambda a: jnp.pad(a, ((0, 0), (0, PAGE_SIZE - t_len), (0, 0)))
    bias_rows = jnp.broadcast_to(jnp.repeat(w['sb_bias'], t_len)[:, None], (SB_HEADS * t_len, PAGE_SIZE))
    ob = _sb_sample(q, bias_rows, pad_rows(k_new), pad_rows(v_new), cache_k, cache_v, page_table, layer=layer)
    ob = ob.reshape(m, D_SB)
    gcols = pad_rows(p3[:, :, GDN_COL0 * 128:]).reshape(nb * PAGE_SIZE, 4 * D_GDN)
    gtail = pad_rows(tail.reshape(nb, t_len, 128)).reshape(nb * PAGE_SIZE, 128)
    yc_pad, s_new = _gdn(gcols, gtail, 0, w['conv_gdn_w'], gdn_buf, w['gdn_a_log'], w['gdn_dt_bias'], gdn_s,
                         w['g_gdn_norm'], nb=nb, t_len=PAGE_SIZE, rows=PAGE_SIZE, chunk=PAGE_SIZE,
                         valid_rows=t_len)
    yc = yc_pad.reshape(nb, PAGE_SIZE, D_GDN)[:, :t_len].reshape(m, D_GDN)
    x1 = _out_proj(ya, ob, yc, x, mod_bm, w['g_grp_sb'], w['g_post_mix'], w['w_out'], tm=m, tiles_per_group=1)
    to_tm = lambda a: a.reshape(nb, -1, a.shape[-1]).transpose(1, 0, 2).reshape(-1, a.shape[-1])
    to_bm = lambda a: a.reshape(-1, nb, a.shape[-1]).transpose(1, 0, 2)
    x2_tm, ffn_conv_tm = _ffn(to_tm(x1), w['g_pre_ffn'], mod_tm, w['g_post_ffn'], w['w_ffn_up'], w['w_ffn_down'],
                              w['conv_ffn_w'], w['conv_ffn_b'], to_tm(ffn_buf)[None],
                              tm=m, rpt=nb, tiles_per_group=1, tf=512)
    x2 = to_bm(x2_tm).reshape(m, D_MODEL)
    ffn_conv = to_bm(ffn_conv_tm[0])
    lru_conv = p3[:, t_len - (CONV_W - 1):, 0:D_LRU]
    gdn_conv = p3[:, t_len - (CONV_W - 1):, GDN_COL0 * 128:GDN_COL0 * 128 + 3 * D_GDN]
    hd = (nb, t_len, SB_HEADS, SB_HEAD_DIM)
    return x2, (k_new.reshape(hd), v_new.reshape(hd), lru_conv, h_last.reshape(nb, D_LRU), gdn_conv, s_new, ffn_conv)


def kernel(x_prompt, x_sample, c_prompt, c_sample, cache_sb_k, cache_sb_v, page_table, state_lru_conv, state_lru_h, state_gdn_conv, state_gdn, state_ffn_conv, w_ada, b_ada, g_pre_mix, g_post_mix, g_pre_ffn, g_post_ffn, w_in, conv_lru_w, conv_lru_b, w_lru_r, b_lru_r, w_lru_i, b_lru_i, lru_lambda, g_grp_lru, g_grp_sb, sb_bias, conv_gdn_w, gdn_a_log, gdn_dt_bias, g_gdn_norm, w_out, w_ffn_up, conv_ffn_w, conv_ffn_b, w_ffn_down):
    bp, t_p, d = x_prompt.shape
    bs, t_s, _ = x_sample.shape
    depth = w_ada.shape[0]
    n_pool = cache_sb_k.shape[1]
    assert bp + bs <= MOD_ROWS and d == D_MODEL

    c_all = jnp.concatenate([c_prompt, c_sample, jnp.zeros((MOD_ROWS - bp - bs, d), F32)], axis=0)
    mod_all = _ada(c_all, w_ada, b_ada).reshape(depth, MOD_ROWS, N_MOD, d)
    cache_k = cache_sb_k.reshape(depth, n_pool, PAGE_SIZE, D_SB)
    cache_v = cache_sb_v.reshape(depth, n_pool, PAGE_SIZE, D_SB)

    params = dict(w_in=w_in, g_pre_mix=g_pre_mix, g_post_mix=g_post_mix, g_pre_ffn=g_pre_ffn, g_post_ffn=g_post_ffn,
                  conv_lru_w=conv_lru_w, conv_lru_b=conv_lru_b, w_lru_r=w_lru_r, b_lru_r=b_lru_r, w_lru_i=w_lru_i,
                  b_lru_i=b_lru_i, lru_lambda=lru_lambda, g_grp_lru=g_grp_lru, g_grp_sb=g_grp_sb, sb_bias=sb_bias,
                  conv_gdn_w=conv_gdn_w, gdn_a_log=gdn_a_log, gdn_dt_bias=gdn_dt_bias, g_gdn_norm=g_gdn_norm,
                  w_out=w_out, w_ffn_up=w_ffn_up, w_ffn_down=w_ffn_down, conv_ffn_w=conv_ffn_w,
                  conv_ffn_b=conv_ffn_b)

    xp = x_prompt.reshape(bp * t_p, d)
    xs = x_sample.reshape(bs * t_s, d)
    outs_p, outs_s = [], []
    for l in range(depth):
        w = _prep_layer_weights({k: v[l] for k, v in params.items()})
        mod_l = mod_all[l]
        mod_p = mod_l[:bp].transpose(1, 0, 2)[:, :, None, :]
        mod_s = mod_l[bp:bp + bs].transpose(1, 0, 2)
        mod_s_bm = jnp.repeat(mod_s, t_s, axis=1)[:, None]
        mod_s_tm = jnp.tile(mod_s, (1, t_s, 1))[:, None]
        xp, op = _layer_prompt(xp, mod_p, w, nb=bp, t_len=t_p)
        xs, os_ = _layer_sample(xs, mod_s_bm, mod_s_tm, w, l, cache_k, cache_v, page_table,
                                state_lru_conv[l], state_lru_h[l], state_gdn_conv[l], state_gdn[l],
                                state_ffn_conv[l], nb=bs, t_len=t_s)
        outs_p.append(op)
        outs_s.append(os_)

    res = [xp.reshape(bp, t_p, d), xs.reshape(bs, t_s, d)]
    order = (0, 1, 2, 3, 4, 5, 6)
    for idx in order:
        res.append(jnp.stack([o[idx] for o in outs_p]))
        res.append(jnp.stack([o[idx] for o in outs_s]))
    y_p, y_s, kp, ks, vp, vs = res[0], res[1], res[2], res[3], res[4], res[5]
    return (y_p, y_s, kp, vp, ks, vs) + tuple(res[6:])
```

```python
import functools
import math

import jax
import jax.numpy as jnp
from jax import lax
from jax.experimental import pallas as pl
from jax.experimental.pallas import tpu as pltpu

F32 = jnp.float32
BF16 = jnp.bfloat16

D_MODEL = 2048
D_LRU = 512
LRU_BLOCKS = 8
LRU_C = 8.0
CONV_W = 4
D_SB = 512
SB_HEADS = 8
SB_HEAD_DIM = 64
D_GDN = 1024
GDN_HEADS = 8
GDN_HEAD_DIM = 128
D_FF = 5632
FFN_CONV_W = 3
N_MOD = 6
EPS = 1e-6
PAGE_SIZE = 128
D_MAIN = 2 * D_LRU + 3 * D_SB + 4 * D_GDN
D_TAIL = 2 * GDN_HEADS
GDN_COL0 = (2 * D_LRU + 3 * D_SB) // 128

VMEM_LIMIT = 56 * 1024 * 1024
MOD_ROWS = 16
SB_PAGES_PER_STEP = 4
SB_LOCKSTEP = 4
GDN_HEADS_PER_STEP = 4


def _cparams(sem):
    return pltpu.CompilerParams(dimension_semantics=sem, vmem_limit_bytes=VMEM_LIMIT)


def _dot(a, b):
    return jnp.dot(a, b, preferred_element_type=F32)


def _dot_nt(a, b):
    return lax.dot_general(a, b, (((1,), (1,)), ((), ())), preferred_element_type=F32)


def _mm1(a, b):
    return _dot(a.astype(BF16), b.astype(BF16))


def _split(a):
    hi = a.astype(BF16)
    lo = (a - hi.astype(F32)).astype(BF16)
    return hi, lo


def _mm3(a, b):
    a_hi, a_lo = _split(a)
    b_hi, b_lo = _split(b)
    return _dot(a_hi, b_hi) + (_dot(a_hi, b_lo) + _dot(a_lo, b_hi))


def _mm_exact_rhs(a, b_bf16):
    a_hi, a_lo = _split(a)
    return _dot(a_hi, b_bf16) + _dot(a_lo, b_bf16)


def _softplus(z):
    return jnp.maximum(z, 0.0) + jnp.log1p(jnp.exp(-jnp.abs(z)))


def _sigmoid(z):
    return jax.nn.sigmoid(z)


def _rms(x, g):
    return x * lax.rsqrt(jnp.mean(x * x, axis=-1, keepdims=True) + EPS) * g


def _ada_kernel(c_ref, w_ref, b_ref, o_ref):
    c = c_ref[...]
    s = c * _sigmoid(c)
    o_ref[0] = _dot(s.astype(BF16), w_ref[0].astype(BF16)) + b_ref[0]


def _ada(c_all, w_ada, b_ada):
    depth, d, n = w_ada.shape
    tn = 1024
    return pl.pallas_call(
        _ada_kernel,
        grid=(depth, n // tn),
        in_specs=[pl.BlockSpec((MOD_ROWS, d), lambda l, j: (0, 0)),
                  pl.BlockSpec((1, d, tn), lambda l, j: (l, 0, j)),
                  pl.BlockSpec((1, 1, tn), lambda l, j: (l, 0, j))],
        out_specs=pl.BlockSpec((1, MOD_ROWS, tn), lambda l, j: (l, 0, j)),
        out_shape=jax.ShapeDtypeStruct((depth, MOD_ROWS, n), F32),
        compiler_params=_cparams(("arbitrary", "arbitrary")),
        name="ada",
    )(c_all, w_ada, b_ada.reshape(depth, 1, n))


def _in_proj_kernel(x_ref, g_ref, sh_ref, sc_ref, w_ref, wt_ref, wtt_ref, o_ref, ot_ref, ott_ref, h_scr):
    j = pl.program_id(1)

    @pl.when(j == 0)
    def _():
        h = _rms(x_ref[...], g_ref[...]) * (1.0 + sc_ref[0, 0]) + sh_ref[0, 0]
        hb = h.astype(BF16)
        h_scr[...] = hb
        ot_ref[...] = _dot(hb, wt_ref[...])
        ott_ref[...] = _dot_nt(wtt_ref[...], hb)

    o_ref[...] = _dot(h_scr[...], w_ref[...])


def _in_proj(x, g, mod, w_main, w_tail, w_tail_t, *, tm, tiles_per_group):
    m, d = x.shape
    n = w_main.shape[1]
    tn = 512
    mrows = mod.shape[2]
    mod_spec = lambda k: pl.BlockSpec((1, 1, mrows, d), lambda i, j: (k, i // tiles_per_group, 0, 0))
    return pl.pallas_call(
        _in_proj_kernel,
        grid=(m // tm, n // tn),
        in_specs=[pl.BlockSpec((tm, d), lambda i, j: (i, 0)),
                  pl.BlockSpec((1, d), lambda i, j: (0, 0)),
                  mod_spec(0), mod_spec(1),
                  pl.BlockSpec((d, tn), lambda i, j: (0, j)),
                  pl.BlockSpec((d, 128), lambda i, j: (0, 0)),
                  pl.BlockSpec((D_TAIL, d), lambda i, j: (0, 0))],
        out_specs=[pl.BlockSpec((tm, tn), lambda i, j: (i, j)),
                   pl.BlockSpec((tm, 128), lambda i, j: (i, 0)),
                   pl.BlockSpec((D_TAIL, tm), lambda i, j: (0, i))],
        out_shape=[jax.ShapeDtypeStruct((m, n), F32),
                   jax.ShapeDtypeStruct((m, 128), F32),
                   jax.ShapeDtypeStruct((D_TAIL, m), F32)],
        scratch_shapes=[pltpu.VMEM((tm, d), BF16)],
        compiler_params=_cparams(("arbitrary", "arbitrary")),
        name="in_proj",
    )(x, g, mod, mod, w_main, w_tail, w_tail_t)


def _lru_kernel(ax_ref, ag_ref, buf_ref, h0_ref, cw_ref, cb_ref, wr_ref, br_ref, wi_ref, bi_ref, lam_ref, gg_ref,
                y_ref, hl_ref, xp_scr, h_scr, *, rows):
    t = pl.program_id(1)

    @pl.when(t == 0)
    def _():
        xp_scr[5:8, :] = buf_ref[0]
        h_scr[...] = h0_ref[0]

    x = ax_ref[...]
    xp_scr[8:8 + rows, :] = x
    cw = cw_ref[...]
    xc = (xp_scr[5:5 + rows, :] * cw[0:1] + xp_scr[6:6 + rows, :] * cw[1:2]
          + xp_scr[7:7 + rows, :] * cw[2:3] + x * cw[3:4]) + cb_ref[...]
    xp_scr[5:8, :] = xp_scr[5 + rows:8 + rows, :]

    xb = xc.astype(BF16)
    r = _sigmoid(_dot(xb, wr_ref[...]) + br_ref[...])
    ig = _sigmoid(_dot(xb, wi_ref[...]) + bi_ref[...])
    log_a = (LRU_C * r) * (-_softplus(-lam_ref[...]))
    a = jnp.exp(log_a)
    u = jnp.sqrt(1.0 - a * a) * (ig * xc)

    ridx = lax.broadcasted_iota(jnp.int32, a.shape, 0)
    s = 1
    while s < rows:
        a_sh = pltpu.roll(a, s, 0)
        u_sh = pltpu.roll(u, s, 0)
        m = ridx >= s
        u = jnp.where(m, a * u_sh + u, u)
        a = jnp.where(m, a * a_sh, a)
        s *= 2
    h = a * h_scr[...] + u
    h_last = h[rows - 1:rows, :]
    h_scr[...] = h_last
    hl_ref[0] = h_last

    ag = ag_ref[...]
    gelu = 0.5 * ag * (1.0 + jnp.tanh(math.sqrt(2.0 / math.pi) * (ag + 0.044715 * (ag * ag * ag))))
    y_ref[...] = _rms(h * gelu, gg_ref[...])


def _lru(proj, buf, h0, cw, cb, wr, br, wi, bi, lam, gg, *, nb, t_len, rows):
    m = proj.shape[0]
    nt = t_len // rows
    row_map = lambda c: (lambda b, t: (b * nt + t, c))
    vec = lambda: pl.BlockSpec((1, D_LRU), lambda b, t: (0, 0))
    return pl.pallas_call(
        functools.partial(_lru_kernel, rows=rows),
        grid=(nb, nt),
        in_specs=[pl.BlockSpec((rows, D_LRU), row_map(0)),
                  pl.BlockSpec((rows, D_LRU), row_map(1)),
                  pl.BlockSpec((1, CONV_W - 1, D_LRU), lambda b, t: (b, 0, 0)),
                  pl.BlockSpec((1, 1, D_LRU), lambda b, t: (b, 0, 0)),
                  pl.BlockSpec((CONV_W, D_LRU), lambda b, t: (0, 0)),
                  vec(),
                  pl.BlockSpec((D_LRU, D_LRU), lambda b, t: (0, 0)), vec(),
                  pl.BlockSpec((D_LRU, D_LRU), lambda b, t: (0, 0)), vec(),
                  vec(), vec()],
        out_specs=[pl.BlockSpec((rows, D_LRU), lambda b, t: (b * nt + t, 0)),
                   pl.BlockSpec((1, 1, D_LRU), lambda b, t: (b, 0, 0))],
        out_shape=[jax.ShapeDtypeStruct((m, D_LRU), F32),
                   jax.ShapeDtypeStruct((nb, 1, D_LRU), F32)],
        scratch_shapes=[pltpu.VMEM((rows + 8, D_LRU), F32), pltpu.VMEM((1, D_LRU), F32)],
        compiler_params=_cparams(("arbitrary", "arbitrary")),
        name="lru",
    )(proj, proj, buf, h0, cw, cb, wr, br, wi, bi, lam, gg)


def _tri_ext(n):
    row = lax.broadcasted_iota(jnp.int32, (n, n + 128), 0)
    col = lax.broadcasted_iota(jnp.int32, (n, n + 128), 1)
    return jnp.where(col >= n, 1.0, jnp.where(row > col, 1.0, 0.0)).astype(BF16)


def _sb_prompt_kernel(bias_ref, q_ref, k_ref, v_ref, o_ref, car_scr, *, tq):
    qi = pl.program_id(1)
    scale = SB_HEAD_DIM ** -0.5
    tri_ext = _tri_ext(tq)
    row = lax.broadcasted_iota(jnp.int32, (tq, tq), 0)
    col = lax.broadcasted_iota(jnp.int32, (tq, tq), 1)
    causal = col < row
    lo_half = lax.broadcasted_iota(jnp.int32, (tq, 128), 1) < SB_HEAD_DIM
    npair = SB_HEADS // 2

    qm = []
    for pr in range(npair):
        qp = q_ref[:, pr * 128:(pr + 1) * 128] * scale
        qm.append((jnp.where(lo_half, qp, 0.0).astype(BF16), jnp.where(lo_half, 0.0, qp).astype(BF16)))
    o_ref[...] = jnp.zeros_like(o_ref)
    car_scr[...] = jnp.zeros_like(car_scr)

    def blocks(kb, mask):
        start = pl.multiple_of(kb * tq, tq)
        for grp in range(SB_HEADS // SB_LOCKSTEP):
            hs = list(range(grp * SB_LOCKSTEP, (grp + 1) * SB_LOCKSTEP))
            kp, vm = {}, {}
            for pr in sorted({h // 2 for h in hs}):
                cols = slice(pr * 128, (pr + 1) * 128)
                kp[pr] = k_ref[pl.ds(start, tq), cols].astype(BF16)
                vp = v_ref[pl.ds(start, tq), cols]
                vm[pr] = (jnp.where(lo_half, vp, 0.0).astype(BF16), jnp.where(lo_half, 0.0, vp).astype(BF16))
            z = [_dot_nt(qm[h // 2][h % 2], kp[h // 2]) + bias_ref[h] for h in hs]
            sp = [_softplus(x) for x in z]
            lk = [-x if mask is None else jnp.where(mask, -x, 0.0) for x in sp]
            ext = [_mm_exact_rhs(x, tri_ext) for x in lk]
            car = [car_scr[h] for h in hs]
            rep = tq // 128
            w = [jnp.exp((zz - s_) + (e[:, :tq] + jnp.concatenate([c] * rep, axis=1)))
                 for zz, s_, e, c in zip(z, sp, ext, car)]
            if mask is not None:
                w = [jnp.where(mask, x, 0.0) for x in w]
            for h, c, e in zip(hs, car, ext):
                car_scr[h] = c + e[:, tq:]
            o = [_dot(x.astype(BF16), vm[h // 2][h % 2]) for x, h in zip(w, hs)]
            c0 = (hs[0] // 2) * 128
            o_ref[:, c0:c0 + 64 * len(hs)] += jnp.concatenate(
                [o[i] + o[i + 1] for i in range(0, len(hs), 2)], axis=1)

    blocks(qi, causal)

    def body(it, c):
        blocks(qi - 1 - it, None)
        return c

    lax.fori_loop(0, qi, body, 0)


def _sb_prompt(proj, bias, *, nb, t_len, tq):
    m = proj.shape[0]
    nq = t_len // tq
    qcol, kcol, vcol = (2 * D_LRU) // D_SB, (2 * D_LRU) // D_SB + 1, (2 * D_LRU) // D_SB + 2
    return pl.pallas_call(
        functools.partial(_sb_prompt_kernel, tq=tq),
        grid=(nb, nq),
        in_specs=[pl.BlockSpec(memory_space=pltpu.SMEM),
                  pl.BlockSpec((tq, D_SB), lambda b, i: (b * nq + i, qcol)),
                  pl.BlockSpec((t_len, D_SB), lambda b, i: (b, kcol)),
                  pl.BlockSpec((t_len, D_SB), lambda b, i: (b, vcol))],
        out_specs=pl.BlockSpec((tq, D_SB), lambda b, i: (b * nq + i, 0)),
        out_shape=jax.ShapeDtypeStruct((m, D_SB), F32),
        scratch_shapes=[pltpu.VMEM((SB_HEADS, tq, 128), F32)],
        compiler_params=_cparams(("arbitrary", "arbitrary")),
        name="sb_prompt",
    )(bias, proj, proj, proj)


def _sb_sample_kernel(pt_ref, q_ref, bias_ref, kn_ref, vn_ref, *rest, t_new, group):
    del pt_ref
    kc_refs, vc_refs = rest[:group], rest[group:2 * group]
    o_ref, acc_scr, car_scr = rest[2 * group:]
    p = pl.program_id(1)
    rows = SB_HEADS * t_new
    scale = SB_HEAD_DIM ** -0.5
    q = q_ref[0] * scale
    q_rep = jnp.concatenate([q] * SB_HEADS, axis=0)
    rh = lax.broadcasted_iota(jnp.int32, (rows, D_SB), 0) >> int(math.log2(t_new))
    ch = lax.broadcasted_iota(jnp.int32, (rows, D_SB), 1) >> int(math.log2(SB_HEAD_DIM))
    own = rh == ch
    qbd = jnp.where(own, q_rep, 0.0).astype(BF16)
    tri_ext = _tri_ext(PAGE_SIZE)

    def pages(kts, vts, mask):
        bias = bias_ref[...]
        z = [_dot(qbd, kt.astype(BF16)) + bias for kt in kts]
        sp = [_softplus(x) for x in z]
        lk = [-x if mask is None else jnp.where(mask, -x, 0.0) for x in sp]
        ext = [_mm_exact_rhs(x, tri_ext) for x in lk]
        car = car_scr[...]
        upd = None
        for zz, s_, e, vt in zip(z, sp, ext, vts):
            w = jnp.exp((zz - s_) + (e[:, :PAGE_SIZE] + car))
            if mask is not None:
                w = jnp.where(mask, w, 0.0)
            car = car + e[:, PAGE_SIZE:]
            o = _dot_nt(w.astype(BF16), vt.astype(BF16))
            upd = o if upd is None else upd + o
        car_scr[...] = car
        acc_scr[...] += upd

    @pl.when(p == 0)
    def _():
        acc_scr[...] = jnp.zeros_like(acc_scr)
        car_scr[...] = jnp.zeros_like(car_scr)
        qpos = lax.broadcasted_iota(jnp.int32, (rows, PAGE_SIZE), 0) & (t_new - 1)
        kpos = lax.broadcasted_iota(jnp.int32, (rows, PAGE_SIZE), 1)
        pages([kn_ref[0]], [vn_ref[0]], kpos < qpos)

    pages([r[...] for r in kc_refs], [r[...] for r in vc_refs], None)

    @pl.when(p == pl.num_programs(1) - 1)
    def _():
        acc = jnp.where(own, acc_scr[...], 0.0)
        out = acc[0:t_new]
        for h in range(1, SB_HEADS):
            out = out + acc[h * t_new:(h + 1) * t_new]
        o_ref[0] = out


def _sb_sample(q, bias_rows, k_new, v_new, cache_k, cache_v, page_table, *, layer):
    nb, t_new, _ = q.shape
    n_pages = page_table.shape[1]
    rows = SB_HEADS * t_new
    group = SB_PAGES_PER_STEP
    assert n_pages % group == 0

    def cache_spec(g):
        return pl.BlockSpec((None, None, D_SB, PAGE_SIZE),
                            lambda b, p, pt: (layer, pt[b, n_pages - 1 - (group * p + g)], 0, 0))

    grid_spec = pltpu.PrefetchScalarGridSpec(
        num_scalar_prefetch=1,
        grid=(nb, n_pages // group),
        in_specs=[pl.BlockSpec((1, t_new, D_SB), lambda b, p, pt: (b, 0, 0)),
                  pl.BlockSpec((rows, PAGE_SIZE), lambda b, p, pt: (0, 0)),
                  pl.BlockSpec((1, D_SB, PAGE_SIZE), lambda b, p, pt: (b, 0, 0)),
                  pl.BlockSpec((1, D_SB, PAGE_SIZE), lambda b, p, pt: (b, 0, 0))]
                 + [cache_spec(g) for g in range(group)] * 2,
        out_specs=pl.BlockSpec((1, t_new, D_SB), lambda b, p, pt: (b, 0, 0)),
        scratch_shapes=[pltpu.VMEM((rows, D_SB), F32), pltpu.VMEM((rows, 128), F32)],
    )
    return pl.pallas_call(
        functools.partial(_sb_sample_kernel, t_new=t_new, group=group),
        grid_spec=grid_spec,
        out_shape=jax.ShapeDtypeStruct((nb, t_new, D_SB), F32),
        compiler_params=_cparams(("arbitrary", "arbitrary")),
        name="sb_sample",
    )(page_table, q, bias_rows, k_new, v_new, *([cache_k] * group), *([cache_v] * group))


def _unit_lower_inverses(lmats, c, support):
    ii = lax.broadcasted_iota(jnp.int32, (c, c), 0)
    jj = lax.broadcasted_iota(jnp.int32, (c, c), 1)
    eye = jnp.where(ii == jj, 1.0, 0.0)
    base = min(16, c)
    shift = int(math.log2(base))
    same_base = (ii >> shift) == (jj >> shift)
    pws = [jnp.where(same_base, l, 0.0) for l in lmats]
    xs = [eye - p for p in pws]
    n = 2
    while n < min(base, support):
        pws = [_mm3(p, p) for p in pws]
        xs = [x + _mm3(x, p) for x, p in zip(xs, pws)]
        n *= 2
    s = base
    while s < min(c, support):
        sh = int(math.log2(s))
        below = ((ii >> (sh + 1)) == (jj >> (sh + 1))) & ((ii >> sh) != (jj >> sh))
        ys = [_mm3(x, jnp.where(below, l, 0.0)) for x, l in zip(xs, lmats)]
        xs = [x - _mm3(y, x) for x, y in zip(xs, ys)]
        s *= 2
    return xs


def _gdn_kernel(q_ref, k_ref, v_ref, z_ref, tail_ref, cwq_ref, cwk_ref, cwv_ref, bq_ref, bk_ref, bv_ref,
                alog_ref, dtb_ref, s0_ref, gn_ref, y_ref, s_ref, xq_scr, xk_scr, xv_scr,
                *, rows, chunk, valid_rows, hg):
    t = pl.program_id(2)
    dk = GDN_HEAD_DIM

    @pl.when(t == 0)
    def _():
        xq_scr[5:8, :] = bq_ref[0]
        xk_scr[5:8, :] = bk_ref[0]
        xv_scr[5:8, :] = bv_ref[0]
        s_ref[0] = s0_ref[0]

    def conv_silu(x_ref, scr, cw_ref):
        x = x_ref[...]
        scr[8:8 + rows, :] = x
        cw = cw_ref[...]
        y = (scr[5:5 + rows, :] * cw[0:1] + scr[6:6 + rows, :] * cw[1:2]
             + scr[7:7 + rows, :] * cw[2:3] + x * cw[3:4])
        scr[5:8, :] = scr[5 + rows:8 + rows, :]
        return y * _sigmoid(y)

    q_all = conv_silu(q_ref, xq_scr, cwq_ref)
    k_all = conv_silu(k_ref, xk_scr, cwk_ref)
    v_all = conv_silu(v_ref, xv_scr, cwv_ref)

    tail = tail_ref[...]
    lane = lax.broadcasted_iota(jnp.int32, tail.shape, 1)
    lane1 = lax.broadcasted_iota(jnp.int32, (1, 128), 1)
    ii = lax.broadcasted_iota(jnp.int32, (chunk, chunk), 0)
    jj = lax.broadcasted_iota(jnp.int32, (chunk, chunk), 1)
    incl = ii >= jj
    strict = ii > jj
    ltri = jnp.where(incl, 1.0, 0.0).astype(BF16)
    support = chunk
    if valid_rows is not None:
        ok = lax.broadcasted_iota(jnp.int32, (rows, 1), 0) + t * rows < valid_rows
        support = min(chunk, 1 << max(valid_rows - 1, 0).bit_length())

    nchunk = rows // chunk
    heads = list(range(hg))
    chains = [(hh, c) for hh in heads for c in range(nchunk)]
    qn, kn, beta, g = [], [], [], []
    for hh in heads:
        h = pl.program_id(1) * hg + hh
        cs = slice(hh * dk, (hh + 1) * dk)
        q, k = q_all[:, cs], k_all[:, cs]
        qn.append(q * lax.rsqrt(jnp.sum(q * q, axis=-1, keepdims=True) + EPS) * (dk ** -0.5))
        kn.append(k * lax.rsqrt(jnp.sum(k * k, axis=-1, keepdims=True) + EPS))
        b_col = jnp.sum(jnp.where(lane == h, tail, 0.0), axis=1, keepdims=True)
        a_col = jnp.sum(jnp.where(lane == GDN_HEADS + h, tail, 0.0), axis=1, keepdims=True)
        a_log = jnp.sum(jnp.where(lane1 == h, alog_ref[...], 0.0), axis=1, keepdims=True)
        dt_b = jnp.sum(jnp.where(lane1 == h, dtb_ref[...], 0.0), axis=1, keepdims=True)
        b_h = _sigmoid(b_col)
        g_h = -jnp.exp(a_log) * _softplus(a_col + dt_b)
        if valid_rows is not None:
            b_h = jnp.where(ok, b_h, 0.0)
            g_h = jnp.where(ok, g_h, 0.0)
        beta.append(b_h)
        g.append(g_h)

    def per_chain(per_head, lanes=False):
        out = []
        for hh, c in chains:
            a = per_head[hh]
            if lanes:
                a = a[:, hh * dk:(hh + 1) * dk]
            out.append(a[c * chunk:(c + 1) * chunk])
        return out

    g_c, beta_c, q_c, k_c = per_chain(g), per_chain(beta), per_chain(qn), per_chain(kn)
    v_c = per_chain([v_all] * hg, lanes=True)
    gcum = [_mm_exact_rhs_lhs(ltri, jnp.broadcast_to(x, (chunk, dk))) for x in g_c]
    gdiff = [_mm_exact_rhs_lhs(ltri, jnp.where(strict, jnp.broadcast_to(x, (chunk, chunk)), 0.0)) for x in g_c]
    decay = [jnp.where(incl, jnp.exp(x), 0.0) for x in gdiff]
    kb = [a * b for a, b in zip(k_c, beta_c)]
    k_bf = [a.astype(BF16) for a in k_c]
    lmat = [jnp.where(strict, _dot_nt(a.astype(BF16), kk) * d, 0.0) for a, kk, d in zip(kb, k_bf, decay)]
    qk = [(_dot_nt(a.astype(BF16), kk) * d).astype(BF16) for a, kk, d in zip(q_c, k_bf, decay)]
    tinv = [x.astype(BF16) for x in _unit_lower_inverses(lmat, chunk, support)]
    egc = [jnp.exp(x) for x in gcum]
    u = [_dot(ti, (a * b).astype(BF16)) for ti, a, b in zip(tinv, v_c, beta_c)]
    w = [_dot(ti, (a * e).astype(BF16)).astype(BF16) for ti, a, e in zip(tinv, kb, egc)]
    g_last = [x[chunk - 1:chunk, :] for x in gcum]
    qg = [(a * e).astype(BF16) for a, e in zip(q_c, egc)]
    kd_t = [(a * jnp.exp(gl - gc)).T.astype(BF16) for a, gl, gc in zip(k_c, g_last, gcum)]
    eg = [jnp.exp(gl) for gl in g_last]

    gn = gn_ref[...]
    s = [s_ref[0, hh] for hh in heads]
    outs = [[] for _ in heads]
    for c in range(nchunk):
        idx = [hh * nchunk + c for hh in heads]
        sb = [x.astype(BF16) for x in s]
        v_new = [u[i] - _dot(w[i], sb[hh]) for hh, i in zip(heads, idx)]
        o_state = [_dot(qg[i], sb[hh]) for hh, i in zip(heads, idx)]
        vb = [x.astype(BF16) for x in v_new]
        o = [o_state[hh] + _dot(qk[i], vb[hh]) for hh, i in zip(heads, idx)]
        s = [s[hh] * eg[i] + _dot(kd_t[i], vb[hh]) for hh, i in zip(heads, idx)]
        for hh in heads:
            outs[hh].append(_rms(o[hh], gn))
    ys = [jnp.concatenate(o, axis=0) if nchunk > 1 else o[0] for o in outs]
    z_all = z_ref[...]
    y_ref[...] = jnp.concatenate(ys, axis=1) * (z_all * _sigmoid(z_all))
    s_ref[0] = jnp.stack(s, axis=0)


def _mm_exact_rhs_lhs(a_bf16, b):
    b_hi, b_lo = _split(b)
    return _dot(a_bf16, b_hi) + _dot(a_bf16, b_lo)


def _gdn(arr, tail, col0, cw, buf, a_log, dt_bias, s0, gn, *, nb, t_len, rows, chunk, valid_rows):
    m = arr.shape[0]
    nt = t_len // rows
    hd = GDN_HEAD_DIM
    nh = GDN_HEADS
    hg = GDN_HEADS_PER_STEP
    wd = hg * hd
    assert nh % hg == 0 and col0 % hg == 0
    col = lambda base: pl.BlockSpec((rows, wd), lambda b, h, t: (b * nt + t, (col0 + base) // hg + h))
    cwspec = lambda base: pl.BlockSpec((CONV_W, wd), lambda b, h, t: (0, base // hg + h))
    bufspec = lambda base: pl.BlockSpec((1, CONV_W - 1, wd), lambda b, h, t: (b, 0, base // hg + h))
    vec = lambda: pl.BlockSpec((1, 128), lambda b, h, t: (0, 0))
    sspec = lambda: pl.BlockSpec((1, hg, hd, hd), lambda b, h, t: (b, h, 0, 0))
    return pl.pallas_call(
        functools.partial(_gdn_kernel, rows=rows, chunk=chunk, valid_rows=valid_rows, hg=hg),
        grid=(nb, nh // hg, nt),
        in_specs=[col(0), col(nh), col(2 * nh), col(3 * nh),
                  pl.BlockSpec((rows, 128), lambda b, h, t: (b * nt + t, 0)),
                  cwspec(0), cwspec(nh), cwspec(2 * nh),
                  bufspec(0), bufspec(nh), bufspec(2 * nh),
                  vec(), vec(), sspec(), vec()],
        out_specs=[pl.BlockSpec((rows, wd), lambda b, h, t: (b * nt + t, h)), sspec()],
        out_shape=[jax.ShapeDtypeStruct((m, D_GDN), F32),
                   jax.ShapeDtypeStruct((nb, nh, hd, hd), F32)],
        scratch_shapes=[pltpu.VMEM((rows + 8, wd), F32)] * 3,
        compiler_params=_cparams(("arbitrary", "arbitrary", "arbitrary")),
        name="gdn",
    )(arr, arr, arr, arr, tail, cw, cw, cw, buf, buf, buf, a_log, dt_bias, s0, gn)


def _out_proj_kernel(ya_ref, ob_ref, yc_ref, x_ref, gate_ref, gsb_ref, gpost_ref, w_ref, o_ref):
    yb = _rms(ob_ref[...], gsb_ref[...])
    mix = (_dot(ya_ref[...].astype(BF16), w_ref[0:D_LRU, :])
           + _dot(yb.astype(BF16), w_ref[D_LRU:D_LRU + D_SB, :])
           + _dot(yc_ref[...].astype(BF16), w_ref[D_LRU + D_SB:, :]))
    o_ref[...] = x_ref[...] + gate_ref[0, 0] * _rms(mix, gpost_ref[...])


def _out_proj(ya, ob, yc, x, mod, gsb, gpost, w_out, *, tm, tiles_per_group):
    m, d = x.shape
    mrows = mod.shape[2]
    return pl.pallas_call(
        _out_proj_kernel,
        grid=(m // tm,),
        in_specs=[pl.BlockSpec((tm, D_LRU), lambda i: (i, 0)),
                  pl.BlockSpec((tm, D_SB), lambda i: (i, 0)),
                  pl.BlockSpec((tm, D_GDN), lambda i: (i, 0)),
                  pl.BlockSpec((tm, d), lambda i: (i, 0)),
                  pl.BlockSpec((1, 1, mrows, d), lambda i: (2, i // tiles_per_group, 0, 0)),
                  pl.BlockSpec((1, D_SB), lambda i: (0, 0)),
                  pl.BlockSpec((1, d), lambda i: (0, 0)),
                  pl.BlockSpec((d, d), lambda i: (0, 0))],
        out_specs=pl.BlockSpec((tm, d), lambda i: (i, 0)),
        out_shape=jax.ShapeDtypeStruct((m, d), F32),
        compiler_params=_cparams(("arbitrary",)),
        name="out_proj",
    )(ya, ob, yc, x, mod, gsb, gpost, w_out)


def _ffn_kernel(x_ref, g_ref, sh_ref, sc_ref, gate_ref, gpost_ref, wg_ref, wv_ref, wd_ref, cw_ref, cb_ref, cin_ref,
                o_ref, cout_ref, h_scr, acc_scr, gbuf_scr, carry_scr, *, tm, rpt, tiles_per_group):
    i = pl.program_id(0)
    j = pl.program_id(1)
    pad = gbuf_scr.shape[0] - tm
    keep = (FFN_CONV_W - 1) * rpt

    @pl.when(j == 0)
    def _():
        h = _rms(x_ref[...], g_ref[...]) * (1.0 + sc_ref[0, 0]) + sh_ref[0, 0]
        h_scr[...] = h.astype(BF16)
        acc_scr[...] = jnp.zeros_like(acc_scr)

    first = (i % tiles_per_group) == 0

    @pl.when(first)
    def _():
        gbuf_scr[pad - keep:pad, :] = cin_ref[0]

    @pl.when(jnp.logical_not(first))
    def _():
        gbuf_scr[pad - keep:pad, :] = carry_scr[j]

    hb = h_scr[...]
    gate = _dot(hb, wg_ref[...])
    val = _dot(hb, wv_ref[...])
    gbuf_scr[pad:pad + tm, :] = gate
    cw = cw_ref[...]
    gc = (gbuf_scr[pad - 2 * rpt:pad - 2 * rpt + tm, :] * cw[0:1]
          + gbuf_scr[pad - rpt:pad - rpt + tm, :] * cw[1:2] + gate * cw[2:3]) + cb_ref[...]
    last = gbuf_scr[pad + tm - keep:pad + tm, :]
    carry_scr[j] = last
    cout_ref[0] = last
    f = (gc * _sigmoid(gc)) * val
    acc_scr[...] += _dot(f.astype(BF16), wd_ref[...])

    @pl.when(j == pl.num_programs(1) - 1)
    def _():
        o_ref[...] = x_ref[...] + gate_ref[0, 0] * _rms(acc_scr[...], gpost_ref[...])


def _ffn(x, g, mod, gpost, w_up, w_down, cw, cb, carry_in, *, tm, rpt, tiles_per_group, tf):
    m, d = x.shape
    nj = D_FF // tf
    mrows = mod.shape[2]
    keep = (FFN_CONV_W - 1) * rpt
    pad = -(-keep // 8) * 8
    mod_spec = lambda k: pl.BlockSpec((1, 1, mrows, d), lambda i, j: (k, i // tiles_per_group, 0, 0))
    return pl.pallas_call(
        functools.partial(_ffn_kernel, tm=tm, rpt=rpt, tiles_per_group=tiles_per_group),
        grid=(m // tm, nj),
        in_specs=[pl.BlockSpec((tm, d), lambda i, j: (i, 0)),
                  pl.BlockSpec((1, d), lambda i, j: (0, 0)),
                  mod_spec(3), mod_spec(4), mod_spec(5),
                  pl.BlockSpec((1, d), lambda i, j: (0, 0)),
                  pl.BlockSpec((d, tf), lambda i, j: (0, j)),
                  pl.BlockSpec((d, tf), lambda i, j: (0, nj + j)),
                  pl.BlockSpec((tf, d), lambda i, j: (j, 0)),
                  pl.BlockSpec((FFN_CONV_W, tf), lambda i, j: (0, j)),
                  pl.BlockSpec((1, tf), lambda i, j: (0, j)),
                  pl.BlockSpec((1, keep, tf), lambda i, j: (i // tiles_per_group, 0, j))],
        out_specs=[pl.BlockSpec((tm, d), lambda i, j: (i, 0)),
                   pl.BlockSpec((1, keep, tf), lambda i, j: (i, 0, j))],
        out_shape=[jax.ShapeDtypeStruct((m, d), F32),
                   jax.ShapeDtypeStruct((m // tm, keep, D_FF), F32)],
        scratch_shapes=[pltpu.VMEM((tm, d), BF16), pltpu.VMEM((tm, d), F32),
                        pltpu.VMEM((pad + tm, tf), F32), pltpu.VMEM((nj, keep, tf), F32)],
        compiler_params=_cparams(("arbitrary", "arbitrary")),
        name="ffn",
    )(x, g, mod, mod, mod, gpost, w_up, w_up, w_down, cw, cb, carry_in)


def _block_diag(w):
    nblk, bw, _ = w.shape
    eye = jnp.eye(nblk, dtype=w.dtype)
    return (eye[:, None, :, None] * w[:, :, None, :]).reshape(nblk * bw, nblk * bw)


def _prep_layer_weights(p):
    w_in = p['w_in']
    row = lambda a: a.reshape(1, -1)
    pad_lanes = lambda a: jnp.pad(a.reshape(1, -1), ((0, 0), (0, 128 - a.shape[-1])))
    return {
        'w_main': w_in[:, :D_MAIN].astype(BF16),
        'w_tail': jnp.pad(w_in[:, D_MAIN:], ((0, 0), (0, 128 - D_TAIL))).astype(BF16),
        'w_tail_t': w_in[:, D_MAIN:].T.astype(BF16),
        'g_pre_mix': row(p['g_pre_mix']), 'g_post_mix': row(p['g_post_mix']),
        'g_pre_ffn': row(p['g_pre_ffn']), 'g_post_ffn': row(p['g_post_ffn']),
        'conv_lru_w': p['conv_lru_w'], 'conv_lru_b': row(p['conv_lru_b']),
        'w_lru_r': _block_diag(p['w_lru_r']).astype(BF16), 'b_lru_r': row(p['b_lru_r']),
        'w_lru_i': _block_diag(p['w_lru_i']).astype(BF16), 'b_lru_i': row(p['b_lru_i']),
        'lru_lambda': row(p['lru_lambda']), 'g_grp_lru': row(p['g_grp_lru']), 'g_grp_sb': row(p['g_grp_sb']),
        'sb_bias': p['sb_bias'],
        'conv_gdn_w': p['conv_gdn_w'],
        'gdn_a_log': pad_lanes(p['gdn_a_log']), 'gdn_dt_bias': pad_lanes(p['gdn_dt_bias']),
        'g_gdn_norm': row(p['g_gdn_norm']),
        'w_out': p['w_out'].astype(BF16),
        'w_ffn_up': p['w_ffn_up'].astype(BF16), 'w_ffn_down': p['w_ffn_down'].astype(BF16),
        'conv_ffn_w': p['conv_ffn_w'], 'conv_ffn_b': row(p['conv_ffn_b']),
    }


def _layer_prompt(x, mod, w, *, nb, t_len):
    tm = 512
    tpg = t_len // tm
    proj, tail, _ = _in_proj(x, w['g_pre_mix'], mod, w['w_main'], w['w_tail'], w['w_tail_t'],
                             tm=tm, tiles_per_group=tpg)
    zeros = lambda *s: jnp.zeros(s, F32)
    ya, h_last = _lru(proj, zeros(nb, CONV_W - 1, D_LRU), zeros(nb, 1, D_LRU), w['conv_lru_w'], w['conv_lru_b'],
                      w['w_lru_r'], w['b_lru_r'], w['w_lru_i'], w['b_lru_i'], w['lru_lambda'], w['g_grp_lru'],
                      nb=nb, t_len=t_len, rows=256)
    ob = _sb_prompt(proj, w['sb_bias'], nb=nb, t_len=t_len, tq=256)
    yc, s_new = _gdn(proj, tail, GDN_COL0, w['conv_gdn_w'], zeros(nb, CONV_W - 1, 3 * D_GDN),
                     w['gdn_a_log'], w['gdn_dt_bias'], zeros(nb, GDN_HEADS, GDN_HEAD_DIM, GDN_HEAD_DIM),
                     w['g_gdn_norm'], nb=nb, t_len=t_len, rows=256, chunk=128, valid_rows=None)
    x1 = _out_proj(ya, ob, yc, x, mod, w['g_grp_sb'], w['g_post_mix'], w['w_out'], tm=256,
                   tiles_per_group=t_len // 256)
    x2, ffn_conv = _ffn(x1, w['g_pre_ffn'], mod, w['g_post_ffn'], w['w_ffn_up'], w['w_ffn_down'],
                        w['conv_ffn_w'], w['conv_ffn_b'], zeros(nb, FFN_CONV_W - 1, D_FF),
                        tm=tm, rpt=1, tiles_per_group=tpg, tf=512)
    ffn_conv = ffn_conv[tpg - 1::tpg]
    p3 = proj.reshape(nb, t_len, D_MAIN)
    k_new = p3[:, :, 3 * D_SB:4 * D_SB].reshape(nb, t_len, SB_HEADS, SB_HEAD_DIM)
    v_new = p3[:, :, 4 * D_SB:5 * D_SB].reshape(nb, t_len, SB_HEADS, SB_HEAD_DIM)
    lru_conv = p3[:, t_len - (CONV_W - 1):, 0:D_LRU]
    gdn_conv = p3[:, t_len - (CONV_W - 1):, GDN_COL0 * 128:GDN_COL0 * 128 + 3 * D_GDN]
    return x2, (k_new, v_new, lru_conv, h_last.reshape(nb, D_LRU), gdn_conv, s_new, ffn_conv)


def _layer_sample(x, mod_bm, mod_tm, w, layer, cache_k, cache_v, page_table, lru_buf, lru_h, gdn_buf, gdn_s,
                  ffn_buf, *, nb, t_len):
    m = nb * t_len
    proj, tail, _ = _in_proj(x, w['g_pre_mix'], mod_bm, w['w_main'], w['w_tail'], w['w_tail_t'],
                             tm=m, tiles_per_group=1)
    ya, h_last = _lru(proj, lru_buf, lru_h.reshape(nb, 1, D_LRU), w['conv_lru_w'], w['conv_lru_b'],
                      w['w_lru_r'], w['b_lru_r'], w['w_lru_i'], w['b_lru_i'], w['lru_lambda'], w['g_grp_lru'],
                      nb=nb, t_len=t_len, rows=t_len)
    p3 = proj.reshape(nb, t_len, D_MAIN)
    q = p3[:, :, 2 * D_SB:3 * D_SB]
    k_new = p3[:, :, 3 * D_SB:4 * D_SB]
    v_new = p3[:, :, 4 * D_SB:5 * D_SB]
    pad_rows = lambda a: jnp.pad(a, ((0, 0), (0, PAGE_SIZE - t_len), (0, 0)))
    bias_rows = jnp.broadcast_to(jnp.repeat(w['sb_bias'], t_len)[:, None], (SB_HEADS * t_len, PAGE_SIZE))
    as_page = lambda a: pad_rows(a).transpose(0, 2, 1)
    ob = _sb_sample(q, bias_rows, as_page(k_new), as_page(v_new), cache_k, cache_v, page_table, layer=layer)
    ob = ob.reshape(m, D_SB)
    gcols = pad_rows(p3[:, :, GDN_COL0 * 128:]).reshape(nb * PAGE_SIZE, 4 * D_GDN)
    gtail = pad_rows(tail.reshape(nb, t_len, 128)).reshape(nb * PAGE_SIZE, 128)
    yc_pad, s_new = _gdn(gcols, gtail, 0, w['conv_gdn_w'], gdn_buf, w['gdn_a_log'], w['gdn_dt_bias'], gdn_s,
                         w['g_gdn_norm'], nb=nb, t_len=PAGE_SIZE, rows=PAGE_SIZE, chunk=PAGE_SIZE,
                         valid_rows=t_len)
    yc = yc_pad.reshape(nb, PAGE_SIZE, D_GDN)[:, :t_len].reshape(m, D_GDN)
    x1 = _out_proj(ya, ob, yc, x, mod_bm, w['g_grp_sb'], w['g_post_mix'], w['w_out'], tm=m, tiles_per_group=1)
    to_tm = lambda a: a.reshape(nb, -1, a.shape[-1]).transpose(1, 0, 2).reshape(-1, a.shape[-1])
    to_bm = lambda a: a.reshape(-1, nb, a.shape[-1]).transpose(1, 0, 2)
    x2_tm, ffn_conv_tm = _ffn(to_tm(x1), w['g_pre_ffn'], mod_tm, w['g_post_ffn'], w['w_ffn_up'], w['w_ffn_down'],
                              w['conv_ffn_w'], w['conv_ffn_b'], to_tm(ffn_buf)[None],
                              tm=m, rpt=nb, tiles_per_group=1, tf=512)
    x2 = to_bm(x2_tm).reshape(m, D_MODEL)
    ffn_conv = to_bm(ffn_conv_tm[0])
    lru_conv = p3[:, t_len - (CONV_W - 1):, 0:D_LRU]
    gdn_conv = p3[:, t_len - (CONV_W - 1):, GDN_COL0 * 128:GDN_COL0 * 128 + 3 * D_GDN]
    hd = (nb, t_len, SB_HEADS, SB_HEAD_DIM)
    return x2, (k_new.reshape(hd), v_new.reshape(hd), lru_conv, h_last.reshape(nb, D_LRU), gdn_conv, s_new, ffn_conv)


def kernel(x_prompt, x_sample, c_prompt, c_sample, cache_sb_k, cache_sb_v, page_table, state_lru_conv, state_lru_h, state_gdn_conv, state_gdn, state_ffn_conv, w_ada, b_ada, g_pre_mix, g_post_mix, g_pre_ffn, g_post_ffn, w_in, conv_lru_w, conv_lru_b, w_lru_r, b_lru_r, w_lru_i, b_lru_i, lru_lambda, g_grp_lru, g_grp_sb, sb_bias, conv_gdn_w, gdn_a_log, gdn_dt_bias, g_gdn_norm, w_out, w_ffn_up, conv_ffn_w, conv_ffn_b, w_ffn_down):
    bp, t_p, d = x_prompt.shape
    bs, t_s, _ = x_sample.shape
    depth = w_ada.shape[0]
    n_pool = cache_sb_k.shape[1]
    assert bp + bs <= MOD_ROWS and d == D_MODEL

    c_all = jnp.concatenate([c_prompt, c_sample, jnp.zeros((MOD_ROWS - bp - bs, d), F32)], axis=0)
    mod_all = _ada(c_all, w_ada, b_ada).reshape(depth, MOD_ROWS, N_MOD, d)
    as_pages = lambda c: c.transpose(0, 1, 3, 4, 2).reshape(depth, n_pool, D_SB, PAGE_SIZE)
    cache_k = as_pages(cache_sb_k)
    cache_v = as_pages(cache_sb_v)

    params = dict(w_in=w_in, g_pre_mix=g_pre_mix, g_post_mix=g_post_mix, g_pre_ffn=g_pre_ffn, g_post_ffn=g_post_ffn,
                  conv_lru_w=conv_lru_w, conv_lru_b=conv_lru_b, w_lru_r=w_lru_r, b_lru_r=b_lru_r, w_lru_i=w_lru_i,
                  b_lru_i=b_lru_i, lru_lambda=lru_lambda, g_grp_lru=g_grp_lru, g_grp_sb=g_grp_sb, sb_bias=sb_bias,
                  conv_gdn_w=conv_gdn_w, gdn_a_log=gdn_a_log, gdn_dt_bias=gdn_dt_bias, g_gdn_norm=g_gdn_norm,
                  w_out=w_out, w_ffn_up=w_ffn_up, w_ffn_down=w_ffn_down, conv_ffn_w=conv_ffn_w,
                  conv_ffn_b=conv_ffn_b)

    xp = x_prompt.reshape(bp * t_p, d)
    xs = x_sample.reshape(bs * t_s, d)
    outs_p, outs_s = [], []
    for l in range(depth):
        w = _prep_layer_weights({k: v[l] for k, v in params.items()})
        mod_l = mod_all[l]
        mod_p = mod_l[:bp].transpose(1, 0, 2)[:, :, None, :]
        mod_s = mod_l[bp:bp + bs].transpose(1, 0, 2)
        mod_s_bm = jnp.repeat(mod_s, t_s, axis=1)[:, None]
        mod_s_tm = jnp.tile(mod_s, (1, t_s, 1))[:, None]
        xp, op = _layer_prompt(xp, mod_p, w, nb=bp, t_len=t_p)
        xs, os_ = _layer_sample(xs, mod_s_bm, mod_s_tm, w, l, cache_k, cache_v, page_table,
                                state_lru_conv[l], state_lru_h[l], state_gdn_conv[l], state_gdn[l],
                                state_ffn_conv[l], nb=bs, t_len=t_s)
        outs_p.append(op)
        outs_s.append(os_)

    res = [xp.reshape(bp, t_p, d), xs.reshape(bs, t_s, d)]
    order = (0, 1, 2, 3, 4, 5, 6)
    for idx in order:
        res.append(jnp.stack([o[idx] for o in outs_p]))
        res.append(jnp.stack([o[idx] for o in outs_s]))
    y_p, y_s, kp, ks, vp, vs = res[0], res[1], res[2], res[3], res[4], res[5]
    return (y_p, y_s, kp, vp, ks, vs) + tuple(res[6:])
```

```python
import functools
import math

import jax
import jax.numpy as jnp
from jax import lax
from jax.experimental import pallas as pl
from jax.experimental.pallas import tpu as pltpu

F32 = jnp.float32
BF16 = jnp.bfloat16

D_MODEL = 2048
D_LRU = 512
LRU_BLOCKS = 8
LRU_C = 8.0
CONV_W = 4
D_SB = 512
SB_HEADS = 8
SB_HEAD_DIM = 64
D_GDN = 1024
GDN_HEADS = 8
GDN_HEAD_DIM = 128
D_FF = 5632
FFN_CONV_W = 3
N_MOD = 6
EPS = 1e-6
PAGE_SIZE = 128
D_MAIN = 2 * D_LRU + 3 * D_SB + 4 * D_GDN
D_TAIL = 2 * GDN_HEADS
GDN_COL0 = (2 * D_LRU + 3 * D_SB) // 128

VMEM_LIMIT = 56 * 1024 * 1024
MOD_ROWS = 16
SB_PAGES_PER_STEP = 8
SB_LOCKSTEP = 8
GDN_HEADS_PER_STEP = 4


def _cparams(sem):
    return pltpu.CompilerParams(dimension_semantics=sem, vmem_limit_bytes=VMEM_LIMIT)


def _dot(a, b):
    return jnp.dot(a, b, preferred_element_type=F32)


def _dot_nt(a, b):
    return lax.dot_general(a, b, (((1,), (1,)), ((), ())), preferred_element_type=F32)


def _split(a):
    hi = a.astype(BF16)
    lo = (a - hi.astype(F32)).astype(BF16)
    return hi, lo


def _mm3s(a, b):
    return _dot(a[0], b[0]) + (_dot(a[0], b[1]) + _dot(a[1], b[0]))


LOG2E = 1.4426950408889634


def _softplus(z):
    return jnp.maximum(z, 0.0) + jnp.log(1.0 + jnp.exp2(jnp.abs(z) * (-LOG2E)))


def _sigmoid(z):
    return jax.nn.sigmoid(z)


def _rms(x, g):
    return x * lax.rsqrt(jnp.mean(x * x, axis=-1, keepdims=True) + EPS) * g


def _ada_kernel(c_ref, w_ref, b_ref, o_ref):
    c = c_ref[...]
    s = c * _sigmoid(c)
    o_ref[0] = _dot(s.astype(BF16), w_ref[0].astype(BF16)) + b_ref[0]


def _ada(c_all, w_ada, b_ada):
    depth, d, n = w_ada.shape
    tn = 1024
    return pl.pallas_call(
        _ada_kernel,
        grid=(depth, n // tn),
        in_specs=[pl.BlockSpec((MOD_ROWS, d), lambda l, j: (0, 0)),
                  pl.BlockSpec((1, d, tn), lambda l, j: (l, 0, j)),
                  pl.BlockSpec((1, 1, tn), lambda l, j: (l, 0, j))],
        out_specs=pl.BlockSpec((1, MOD_ROWS, tn), lambda l, j: (l, 0, j)),
        out_shape=jax.ShapeDtypeStruct((depth, MOD_ROWS, n), F32),
        compiler_params=_cparams(("arbitrary", "arbitrary")),
        name="ada",
    )(c_all, w_ada, b_ada.reshape(depth, 1, n))


def _in_proj_kernel(x_ref, g_ref, sh_ref, sc_ref, w_ref, wt_ref, o_ref, ot_ref, h_scr):
    j = pl.program_id(1)

    @pl.when(j == 0)
    def _():
        h = _rms(x_ref[...], g_ref[...]) * (1.0 + sc_ref[0, 0]) + sh_ref[0, 0]
        hb = h.astype(BF16)
        h_scr[...] = hb
        ot_ref[...] = _dot(hb, wt_ref[...])

    o_ref[...] = _dot(h_scr[...], w_ref[...])


def _in_proj(x, g, mod, w_all, w_tail, *, tm, tiles_per_group):
    m, d = x.shape
    n = D_MAIN
    tn = 512
    mrows = mod.shape[2]
    mod_spec = lambda k: pl.BlockSpec((1, 1, mrows, d), lambda i, j: (k, i // tiles_per_group, 0, 0))
    return pl.pallas_call(
        _in_proj_kernel,
        grid=(m // tm, n // tn),
        in_specs=[pl.BlockSpec((tm, d), lambda i, j: (i, 0)),
                  pl.BlockSpec((1, d), lambda i, j: (0, 0)),
                  mod_spec(0), mod_spec(1),
                  pl.BlockSpec((d, tn), lambda i, j: (0, j)),
                  pl.BlockSpec((d, 128), lambda i, j: (0, 0))],
        out_specs=[pl.BlockSpec((tm, tn), lambda i, j: (i, j)),
                   pl.BlockSpec((tm, 128), lambda i, j: (i, 0))],
        out_shape=[jax.ShapeDtypeStruct((m, n), F32),
                   jax.ShapeDtypeStruct((m, 128), F32)],
        scratch_shapes=[pltpu.VMEM((tm, d), BF16)],
        compiler_params=_cparams(("arbitrary", "arbitrary")),
        name="in_proj",
    )(x, g, mod, mod, w_all, w_tail)


def _lru_kernel(ax_ref, ag_ref, buf_ref, h0_ref, cw_ref, cb_ref, wr_ref, br_ref, wi_ref, bi_ref, lam_ref, gg_ref,
                y_ref, hl_ref, xp_scr, h_scr, *, rows):
    t = pl.program_id(1)

    @pl.when(t == 0)
    def _():
        xp_scr[5:8, :] = buf_ref[0]
        h_scr[...] = h0_ref[0]

    x = ax_ref[...]
    xp_scr[8:8 + rows, :] = x
    cw = cw_ref[...]
    xc = (xp_scr[5:5 + rows, :] * cw[0:1] + xp_scr[6:6 + rows, :] * cw[1:2]
          + xp_scr[7:7 + rows, :] * cw[2:3] + x * cw[3:4]) + cb_ref[...]
    xp_scr[5:8, :] = xp_scr[5 + rows:8 + rows, :]

    xb = xc.astype(BF16)
    r = _sigmoid(_dot(xb, wr_ref[...]) + br_ref[...])
    ig = _sigmoid(_dot(xb, wi_ref[...]) + bi_ref[...])
    log_a = (LRU_C * r) * (-_softplus(-lam_ref[...]))
    a = jnp.exp(log_a)
    u = jnp.sqrt(1.0 - a * a) * (ig * xc)

    ridx = lax.broadcasted_iota(jnp.int32, a.shape, 0)
    s = 1
    while s < rows:
        a_sh = pltpu.roll(a, s, 0)
        u_sh = pltpu.roll(u, s, 0)
        m = ridx >= s
        u = jnp.where(m, a * u_sh + u, u)
        a = jnp.where(m, a * a_sh, a)
        s *= 2
    h = a * h_scr[...] + u
    h_last = h[rows - 1:rows, :]
    h_scr[...] = h_last
    hl_ref[0] = h_last

    ag = ag_ref[...]
    gelu = 0.5 * ag * (1.0 + jnp.tanh(math.sqrt(2.0 / math.pi) * (ag + 0.044715 * (ag * ag * ag))))
    y_ref[...] = _rms(h * gelu, gg_ref[...])


def _lru(proj, buf, h0, cw, cb, wr, br, wi, bi, lam, gg, *, nb, t_len, rows):
    m = proj.shape[0]
    nt = t_len // rows
    row_map = lambda c: (lambda b, t: (b * nt + t, c))
    vec = lambda: pl.BlockSpec((1, D_LRU), lambda b, t: (0, 0))
    return pl.pallas_call(
        functools.partial(_lru_kernel, rows=rows),
        grid=(nb, nt),
        in_specs=[pl.BlockSpec((rows, D_LRU), row_map(0)),
                  pl.BlockSpec((rows, D_LRU), row_map(1)),
                  pl.BlockSpec((1, CONV_W - 1, D_LRU), lambda b, t: (b, 0, 0)),
                  pl.BlockSpec((1, 1, D_LRU), lambda b, t: (b, 0, 0)),
                  pl.BlockSpec((CONV_W, D_LRU), lambda b, t: (0, 0)),
                  vec(),
                  pl.BlockSpec((D_LRU, D_LRU), lambda b, t: (0, 0)), vec(),
                  pl.BlockSpec((D_LRU, D_LRU), lambda b, t: (0, 0)), vec(),
                  vec(), vec()],
        out_specs=[pl.BlockSpec((rows, D_LRU), lambda b, t: (b * nt + t, 0)),
                   pl.BlockSpec((1, 1, D_LRU), lambda b, t: (b, 0, 0))],
        out_shape=[jax.ShapeDtypeStruct((m, D_LRU), F32),
                   jax.ShapeDtypeStruct((nb, 1, D_LRU), F32)],
        scratch_shapes=[pltpu.VMEM((rows + 8, D_LRU), F32), pltpu.VMEM((1, D_LRU), F32)],
        compiler_params=_cparams(("arbitrary", "arbitrary")),
        name="lru",
    )(proj, proj, buf, h0, cw, cb, wr, br, wi, bi, lam, gg)


def _neg_tri2(n):
    row = lax.broadcasted_iota(jnp.int32, (2 * n, n), 0) & (n - 1)
    col = lax.broadcasted_iota(jnp.int32, (2 * n, n), 1)
    return jnp.where(row >= col, -1.0, 0.0).astype(BF16)


def _sb_stages(zs, mask, ntri2, carries):
    ss = [_softplus(z) for z in zs]
    if mask is not None:
        ss = [jnp.where(mask, s, 0.0) for s in ss]
    parts = [_split(s) for s in ss]
    after = [_dot(jnp.concatenate([hi, lo], axis=1), ntri2) for hi, lo in parts]
    ws = [jnp.exp(z + (a + c)) for z, a, c in zip(zs, after, carries)]
    if mask is not None:
        ws = [jnp.where(mask, w, 0.0) for w in ws]
    return ws, [jnp.sum(s, axis=1, keepdims=True) for s in ss]


def _sb_prompt_kernel(bias_ref, q_ref, k_ref, v_ref, o_ref, car_scr, *, tq):
    qi = pl.program_id(1)
    scale = SB_HEAD_DIM ** -0.5
    ntri2 = _neg_tri2(tq)
    row = lax.broadcasted_iota(jnp.int32, (tq, tq), 0)
    col = lax.broadcasted_iota(jnp.int32, (tq, tq), 1)
    causal = col < row
    lo_half = lax.broadcasted_iota(jnp.int32, (tq, 128), 1) < SB_HEAD_DIM
    npair = SB_HEADS // 2

    qm = []
    for pr in range(npair):
        qp = q_ref[:, pr * 128:(pr + 1) * 128] * scale
        qm.append((jnp.where(lo_half, qp, 0.0).astype(BF16), jnp.where(lo_half, 0.0, qp).astype(BF16)))
    o_ref[...] = jnp.zeros_like(o_ref)
    car_scr[...] = jnp.zeros_like(car_scr)

    def blocks(kb, mask):
        start = pl.multiple_of(kb * tq, tq)
        for grp in range(SB_HEADS // SB_LOCKSTEP):
            hs = list(range(grp * SB_LOCKSTEP, (grp + 1) * SB_LOCKSTEP))
            kp, vm = {}, {}
            for pr in sorted({h // 2 for h in hs}):
                cols = slice(pr * 128, (pr + 1) * 128)
                kp[pr] = k_ref[pl.ds(start, tq), cols].astype(BF16)
                vp = v_ref[pl.ds(start, tq), cols]
                vm[pr] = (jnp.where(lo_half, vp, 0.0).astype(BF16), jnp.where(lo_half, 0.0, vp).astype(BF16))
            z = [_dot_nt(qm[h // 2][h % 2], kp[h // 2]) + bias_ref[h] for h in hs]
            car = [car_scr[h] for h in hs]
            w, tot = _sb_stages(z, mask, ntri2, car)
            for h, c, t_ in zip(hs, car, tot):
                car_scr[h] = c - t_
            o = [_dot(x.astype(BF16), vm[h // 2][h % 2]) for x, h in zip(w, hs)]
            c0 = (hs[0] // 2) * 128
            o_ref[:, c0:c0 + 64 * len(hs)] += jnp.concatenate(
                [o[i] + o[i + 1] for i in range(0, len(hs), 2)], axis=1)

    blocks(qi, causal)

    def body(it, c):
        blocks(qi - 1 - it, None)
        return c

    lax.fori_loop(0, qi, body, 0)


def _sb_prompt(proj, bias, *, nb, t_len, tq):
    m = proj.shape[0]
    nq = t_len // tq
    qcol, kcol, vcol = (2 * D_LRU) // D_SB, (2 * D_LRU) // D_SB + 1, (2 * D_LRU) // D_SB + 2
    return pl.pallas_call(
        functools.partial(_sb_prompt_kernel, tq=tq),
        grid=(nb, nq),
        in_specs=[pl.BlockSpec(memory_space=pltpu.SMEM),
                  pl.BlockSpec((tq, D_SB), lambda b, i: (b * nq + i, qcol)),
                  pl.BlockSpec((t_len, D_SB), lambda b, i: (b, kcol)),
                  pl.BlockSpec((t_len, D_SB), lambda b, i: (b, vcol))],
        out_specs=pl.BlockSpec((tq, D_SB), lambda b, i: (b * nq + i, 0)),
        out_shape=jax.ShapeDtypeStruct((m, D_SB), F32),
        scratch_shapes=[pltpu.VMEM((SB_HEADS, tq, 1), F32)],
        compiler_params=_cparams(("arbitrary", "arbitrary")),
        name="sb_prompt",
    )(bias, proj, proj, proj)


def _sb_sample_kernel(pt_ref, q_ref, bias_ref, kn_ref, vn_ref, *rest, t_new, group):
    del pt_ref
    kc_refs, vc_refs = rest[:group], rest[group:2 * group]
    o_ref, acc_scr, car_scr = rest[2 * group:]
    p = pl.program_id(1)
    rows = SB_HEADS * t_new
    scale = SB_HEAD_DIM ** -0.5
    q = q_ref[0] * scale
    q_rep = jnp.concatenate([q] * SB_HEADS, axis=0)
    rh = lax.broadcasted_iota(jnp.int32, (rows, D_SB), 0) >> int(math.log2(t_new))
    ch = lax.broadcasted_iota(jnp.int32, (rows, D_SB), 1) >> int(math.log2(SB_HEAD_DIM))
    own = rh == ch
    qbd = jnp.where(own, q_rep, 0.0).astype(BF16)
    ntri2 = _neg_tri2(PAGE_SIZE)

    def pages(kts, vts, mask):
        bias = bias_ref[...]
        z = [_dot(qbd, kt.astype(BF16)) + bias for kt in kts]
        ss = [_softplus(x) for x in z]
        if mask is not None:
            ss = [jnp.where(mask, s, 0.0) for s in ss]
        tot = [jnp.sum(s, axis=1, keepdims=True) for s in ss]
        parts = [_split(s) for s in ss]
        after = [_dot(jnp.concatenate([hi, lo], axis=1), ntri2) for hi, lo in parts]
        car = car_scr[...]
        upd = None
        for zz, a, t_, vt in zip(z, after, tot, vts):
            w = jnp.exp(zz + (a + car))
            if mask is not None:
                w = jnp.where(mask, w, 0.0)
            car = car - t_
            o = _dot_nt(w.astype(BF16), vt.astype(BF16))
            upd = o if upd is None else upd + o
        car_scr[...] = car
        acc_scr[...] += upd

    @pl.when(p == 0)
    def _():
        acc_scr[...] = jnp.zeros_like(acc_scr)
        car_scr[...] = jnp.zeros_like(car_scr)
        qpos = lax.broadcasted_iota(jnp.int32, (rows, PAGE_SIZE), 0) & (t_new - 1)
        kpos = lax.broadcasted_iota(jnp.int32, (rows, PAGE_SIZE), 1)
        pages([kn_ref[0]], [vn_ref[0]], kpos < qpos)

    for g0 in range(0, group, SB_LOCKSTEP):
        sub = slice(g0, g0 + SB_LOCKSTEP)
        pages([r[...] for r in kc_refs[sub]], [r[...] for r in vc_refs[sub]], None)

    @pl.when(p == pl.num_programs(1) - 1)
    def _():
        acc = jnp.where(own, acc_scr[...], 0.0)
        out = acc[0:t_new]
        for h in range(1, SB_HEADS):
            out = out + acc[h * t_new:(h + 1) * t_new]
        o_ref[0] = out


def _sb_sample(q, bias_rows, k_new, v_new, cache_k, cache_v, page_table, *, layer):
    nb, t_new, _ = q.shape
    n_pages = page_table.shape[1]
    rows = SB_HEADS * t_new
    group = SB_PAGES_PER_STEP
    assert n_pages % group == 0

    def cache_spec(g):
        return pl.BlockSpec((None, None, D_SB, PAGE_SIZE),
                            lambda b, p, pt: (layer, pt[b, n_pages - 1 - (group * p + g)], 0, 0))

    grid_spec = pltpu.PrefetchScalarGridSpec(
        num_scalar_prefetch=1,
        grid=(nb, n_pages // group),
        in_specs=[pl.BlockSpec((1, t_new, D_SB), lambda b, p, pt: (b, 0, 0)),
                  pl.BlockSpec((rows, PAGE_SIZE), lambda b, p, pt: (0, 0)),
                  pl.BlockSpec((1, D_SB, PAGE_SIZE), lambda b, p, pt: (b, 0, 0)),
                  pl.BlockSpec((1, D_SB, PAGE_SIZE), lambda b, p, pt: (b, 0, 0))]
                 + [cache_spec(g) for g in range(group)] * 2,
        out_specs=pl.BlockSpec((1, t_new, D_SB), lambda b, p, pt: (b, 0, 0)),
        scratch_shapes=[pltpu.VMEM((rows, D_SB), F32), pltpu.VMEM((rows, 1), F32)],
    )
    return pl.pallas_call(
        functools.partial(_sb_sample_kernel, t_new=t_new, group=group),
        grid_spec=grid_spec,
        out_shape=jax.ShapeDtypeStruct((nb, t_new, D_SB), F32),
        compiler_params=_cparams(("arbitrary", "arbitrary")),
        name="sb_sample",
    )(page_table, q, bias_rows, k_new, v_new, *([cache_k] * group), *([cache_v] * group))


def _unit_lower_inverses(lmats, c, support):
    ii = lax.broadcasted_iota(jnp.int32, (c, c), 0)
    jj = lax.broadcasted_iota(jnp.int32, (c, c), 1)
    eye = jnp.where(ii == jj, 1.0, 0.0)
    base = min(16, c)
    shift = int(math.log2(base))
    same_base = (ii >> shift) == (jj >> shift)
    pws = [jnp.where(same_base, l, 0.0) for l in lmats]
    xs = [eye - p for p in pws]
    ps = [_split(p) for p in pws]
    n = 2
    while n < min(base, support):
        ps = [_split(_mm3s(p, p)) for p in ps]
        xs = [x + _mm3s(_split(x), p) for x, p in zip(xs, ps)]
        n *= 2
    s = base
    while s < min(c, support):
        sh = int(math.log2(s))
        below = ((ii >> (sh + 1)) == (jj >> (sh + 1))) & ((ii >> sh) != (jj >> sh))
        xsp = [_split(x) for x in xs]
        ys = [_mm3s(x, _split(jnp.where(below, l, 0.0))) for x, l in zip(xsp, lmats)]
        xs = [x - _mm3s(_split(y), xp) for x, y, xp in zip(xs, ys, xsp)]
        s *= 2
    return xs


def _gdn_kernel(q_ref, k_ref, v_ref, z_ref, tail_ref, cwq_ref, cwk_ref, cwv_ref, bq_ref, bk_ref, bv_ref,
                alog_ref, dtb_ref, s0_ref, gn_ref, y_ref, s_ref, xq_scr, xk_scr, xv_scr,
                *, rows, chunk, valid_rows, hg):
    t = pl.program_id(2)
    dk = GDN_HEAD_DIM

    @pl.when(t == 0)
    def _():
        xq_scr[5:8, :] = bq_ref[0]
        xk_scr[5:8, :] = bk_ref[0]
        xv_scr[5:8, :] = bv_ref[0]
        s_ref[0] = s0_ref[0]

    def conv_silu(x_ref, scr, cw_ref):
        x = x_ref[...]
        scr[8:8 + rows, :] = x
        cw = cw_ref[...]
        y = (scr[5:5 + rows, :] * cw[0:1] + scr[6:6 + rows, :] * cw[1:2]
             + scr[7:7 + rows, :] * cw[2:3] + x * cw[3:4])
        scr[5:8, :] = scr[5 + rows:8 + rows, :]
        return y * _sigmoid(y)

    q_all = conv_silu(q_ref, xq_scr, cwq_ref)
    k_all = conv_silu(k_ref, xk_scr, cwk_ref)
    v_all = conv_silu(v_ref, xv_scr, cwv_ref)

    tail = tail_ref[...]
    crows = -(-rows // chunk) * chunk
    if crows > rows:
        assert valid_rows is not None and valid_rows <= rows
        pad0 = lambda a: jnp.concatenate([a, jnp.zeros((crows - rows, a.shape[1]), F32)], axis=0)
        q_all, k_all, v_all, tail = pad0(q_all), pad0(k_all), pad0(v_all), pad0(tail)
    lane = lax.broadcasted_iota(jnp.int32, tail.shape, 1)
    lane1 = lax.broadcasted_iota(jnp.int32, (1, 128), 1)
    ii = lax.broadcasted_iota(jnp.int32, (chunk, chunk), 0)
    jj = lax.broadcasted_iota(jnp.int32, (chunk, chunk), 1)
    incl = ii >= jj
    strict = ii > jj
    ltri = jnp.where(incl, 1.0, 0.0).astype(BF16)
    support = chunk
    if valid_rows is not None:
        ok = lax.broadcasted_iota(jnp.int32, (crows, 1), 0) + t * rows < valid_rows
        support = min(chunk, 1 << max(valid_rows - 1, 0).bit_length())

    nchunk = crows // chunk
    heads = list(range(hg))
    chains = [(hh, c) for hh in heads for c in range(nchunk)]
    qn, kn, beta, g = [], [], [], []
    for hh in heads:
        h = pl.program_id(1) * hg + hh
        cs = slice(hh * dk, (hh + 1) * dk)
        q, k = q_all[:, cs], k_all[:, cs]
        qn.append(q * lax.rsqrt(jnp.sum(q * q, axis=-1, keepdims=True) + EPS) * (dk ** -0.5))
        kn.append(k * lax.rsqrt(jnp.sum(k * k, axis=-1, keepdims=True) + EPS))
        b_col = jnp.sum(jnp.where(lane == h, tail, 0.0), axis=1, keepdims=True)
        a_col = jnp.sum(jnp.where(lane == GDN_HEADS + h, tail, 0.0), axis=1, keepdims=True)
        a_log = jnp.sum(jnp.where(lane1 == h, alog_ref[...], 0.0), axis=1, keepdims=True)
        dt_b = jnp.sum(jnp.where(lane1 == h, dtb_ref[...], 0.0), axis=1, keepdims=True)
        b_h = _sigmoid(b_col)
        g_h = -jnp.exp(a_log) * _softplus(a_col + dt_b)
        if valid_rows is not None:
            b_h = jnp.where(ok, b_h, 0.0)
            g_h = jnp.where(ok, g_h, 0.0)
        beta.append(b_h)
        g.append(g_h)

    def per_chain(per_head, lanes=False):
        out = []
        for hh, c in chains:
            a = per_head[hh]
            if lanes:
                a = a[:, hh * dk:(hh + 1) * dk]
            out.append(a[c * chunk:(c + 1) * chunk])
        return out

    g_c, beta_c, q_c, k_c = per_chain(g), per_chain(beta), per_chain(qn), per_chain(kn)
    v_c = per_chain([v_all] * hg, lanes=True)
    gcum = [_mm_exact_rhs_lhs(ltri, jnp.broadcast_to(x, (chunk, dk))) for x in g_c]
    gdiff = [_mm_exact_rhs_lhs(ltri, jnp.where(strict, jnp.broadcast_to(x, (chunk, chunk)), 0.0)) for x in g_c]
    decay = [jnp.where(incl, jnp.exp(x), 0.0) for x in gdiff]
    kb = [a * b for a, b in zip(k_c, beta_c)]
    k_bf = [a.astype(BF16) for a in k_c]
    lmat = [jnp.where(strict, _dot_nt(a.astype(BF16), kk) * d, 0.0) for a, kk, d in zip(kb, k_bf, decay)]
    qk = [(_dot_nt(a.astype(BF16), kk) * d).astype(BF16) for a, kk, d in zip(q_c, k_bf, decay)]
    tinv = [x.astype(BF16) for x in _unit_lower_inverses(lmat, chunk, support)]
    egc = [jnp.exp(x) for x in gcum]
    u = [_dot(ti, (a * b).astype(BF16)) for ti, a, b in zip(tinv, v_c, beta_c)]
    w = [_dot(ti, (a * e).astype(BF16)).astype(BF16) for ti, a, e in zip(tinv, kb, egc)]
    g_last = [x[chunk - 1:chunk, :] for x in gcum]
    qg = [(a * e).astype(BF16) for a, e in zip(q_c, egc)]
    kd_t = [(a * jnp.exp(gl - gc)).T.astype(BF16) for a, gl, gc in zip(k_c, g_last, gcum)]
    eg = [jnp.exp(gl) for gl in g_last]

    gn = gn_ref[...]
    s = [s_ref[0, hh] for hh in heads]
    outs = [[] for _ in heads]
    for c in range(nchunk):
        idx = [hh * nchunk + c for hh in heads]
        sb = [x.astype(BF16) for x in s]
        v_new = [u[i] - _dot(w[i], sb[hh]) for hh, i in zip(heads, idx)]
        o_state = [_dot(qg[i], sb[hh]) for hh, i in zip(heads, idx)]
        vb = [x.astype(BF16) for x in v_new]
        o = [o_state[hh] + _dot(qk[i], vb[hh]) for hh, i in zip(heads, idx)]
        s = [s[hh] * eg[i] + _dot(kd_t[i], vb[hh]) for hh, i in zip(heads, idx)]
        for hh in heads:
            outs[hh].append(_rms(o[hh], gn))
    ys = [jnp.concatenate(o, axis=0) if nchunk > 1 else o[0] for o in outs]
    z_all = z_ref[...]
    y_ref[...] = jnp.concatenate(ys, axis=1)[:rows] * (z_all * _sigmoid(z_all))
    s_ref[0] = jnp.stack(s, axis=0)


def _mm_exact_rhs_lhs(a_bf16, b):
    b_hi, b_lo = _split(b)
    return _dot(a_bf16, b_hi) + _dot(a_bf16, b_lo)


def _gdn(arr, tail, col0, cw, buf, a_log, dt_bias, s0, gn, *, nb, t_len, rows, chunk, valid_rows):
    m = arr.shape[0]
    nt = t_len // rows
    hd = GDN_HEAD_DIM
    nh = GDN_HEADS
    hg = GDN_HEADS_PER_STEP
    wd = hg * hd
    assert nh % hg == 0 and col0 % hg == 0
    col = lambda base: pl.BlockSpec((rows, wd), lambda b, h, t: (b * nt + t, (col0 + base) // hg + h))
    cwspec = lambda base: pl.BlockSpec((CONV_W, wd), lambda b, h, t: (0, base // hg + h))
    bufspec = lambda base: pl.BlockSpec((1, CONV_W - 1, wd), lambda b, h, t: (b, 0, base // hg + h))
    vec = lambda: pl.BlockSpec((1, 128), lambda b, h, t: (0, 0))
    sspec = lambda: pl.BlockSpec((1, hg, hd, hd), lambda b, h, t: (b, h, 0, 0))
    return pl.pallas_call(
        functools.partial(_gdn_kernel, rows=rows, chunk=chunk, valid_rows=valid_rows, hg=hg),
        grid=(nb, nh // hg, nt),
        in_specs=[col(0), col(nh), col(2 * nh), col(3 * nh),
                  pl.BlockSpec((rows, 128), lambda b, h, t: (b * nt + t, 0)),
                  cwspec(0), cwspec(nh), cwspec(2 * nh),
                  bufspec(0), bufspec(nh), bufspec(2 * nh),
                  vec(), vec(), sspec(), vec()],
        out_specs=[pl.BlockSpec((rows, wd), lambda b, h, t: (b * nt + t, h)), sspec()],
        out_shape=[jax.ShapeDtypeStruct((m, D_GDN), F32),
                   jax.ShapeDtypeStruct((nb, nh, hd, hd), F32)],
        scratch_shapes=[pltpu.VMEM((rows + 8, wd), F32)] * 3,
        compiler_params=_cparams(("arbitrary", "arbitrary", "arbitrary")),
        name="gdn",
    )(arr, arr, arr, arr, tail, cw, cw, cw, buf, buf, buf, a_log, dt_bias, s0, gn)


def _out_proj_kernel(ya_ref, ob_ref, yc_ref, x_ref, gate_ref, gsb_ref, gpost_ref, w_ref, o_ref):
    yb = _rms(ob_ref[...], gsb_ref[...])
    mix = (_dot(ya_ref[...].astype(BF16), w_ref[0:D_LRU, :])
           + _dot(yb.astype(BF16), w_ref[D_LRU:D_LRU + D_SB, :])
           + _dot(yc_ref[...].astype(BF16), w_ref[D_LRU + D_SB:, :]))
    o_ref[...] = x_ref[...] + gate_ref[0, 0] * _rms(mix, gpost_ref[...])


def _out_proj(ya, ob, yc, x, mod, gsb, gpost, w_out, *, tm, tiles_per_group):
    m, d = x.shape
    mrows = mod.shape[2]
    return pl.pallas_call(
        _out_proj_kernel,
        grid=(m // tm,),
        in_specs=[pl.BlockSpec((tm, D_LRU), lambda i: (i, 0)),
                  pl.BlockSpec((tm, D_SB), lambda i: (i, 0)),
                  pl.BlockSpec((tm, D_GDN), lambda i: (i, 0)),
                  pl.BlockSpec((tm, d), lambda i: (i, 0)),
                  pl.BlockSpec((1, 1, mrows, d), lambda i: (2, i // tiles_per_group, 0, 0)),
                  pl.BlockSpec((1, D_SB), lambda i: (0, 0)),
                  pl.BlockSpec((1, d), lambda i: (0, 0)),
                  pl.BlockSpec((d, d), lambda i: (0, 0))],
        out_specs=pl.BlockSpec((tm, d), lambda i: (i, 0)),
        out_shape=jax.ShapeDtypeStruct((m, d), F32),
        compiler_params=_cparams(("arbitrary",)),
        name="out_proj",
    )(ya, ob, yc, x, mod, gsb, gpost, w_out)


def _ffn_kernel(x_ref, g_ref, sh_ref, sc_ref, gate_ref, gpost_ref, wg_ref, wv_ref, wd_ref, cw_ref, cb_ref, cin_ref,
                o_ref, cout_ref, h_scr, acc_scr, gbuf_scr, carry_scr, *, tm, rpt, tiles_per_group):
    i = pl.program_id(0)
    j = pl.program_id(1)
    pad = gbuf_scr.shape[0] - tm
    keep = (FFN_CONV_W - 1) * rpt

    @pl.when(j == 0)
    def _():
        h = _rms(x_ref[...], g_ref[...]) * (1.0 + sc_ref[0, 0]) + sh_ref[0, 0]
        h_scr[...] = h.astype(BF16)
        acc_scr[...] = jnp.zeros_like(acc_scr)

    first = (i % tiles_per_group) == 0

    @pl.when(first)
    def _():
        gbuf_scr[pad - keep:pad, :] = cin_ref[0]

    @pl.when(jnp.logical_not(first))
    def _():
        gbuf_scr[pad - keep:pad, :] = carry_scr[j]

    hb = h_scr[...]
    gate = _dot(hb, wg_ref[...])
    val = _dot(hb, wv_ref[...])
    gbuf_scr[pad:pad + tm, :] = gate
    cw = cw_ref[...]
    gc = (gbuf_scr[pad - 2 * rpt:pad - 2 * rpt + tm, :] * cw[0:1]
          + gbuf_scr[pad - rpt:pad - rpt + tm, :] * cw[1:2] + gate * cw[2:3]) + cb_ref[...]
    last = gbuf_scr[pad + tm - keep:pad + tm, :]
    carry_scr[j] = last
    cout_ref[0] = last
    f = (gc * _sigmoid(gc)) * val
    acc_scr[...] += _dot(f.astype(BF16), wd_ref[...])

    @pl.when(j == pl.num_programs(1) - 1)
    def _():
        o_ref[...] = x_ref[...] + gate_ref[0, 0] * _rms(acc_scr[...], gpost_ref[...])


def _ffn(x, g, mod, gpost, w_up, w_down, cw, cb, carry_in, *, tm, rpt, tiles_per_group, tf):
    m, d = x.shape
    nj = D_FF // tf
    mrows = mod.shape[2]
    keep = (FFN_CONV_W - 1) * rpt
    pad = -(-keep // 8) * 8
    mod_spec = lambda k: pl.BlockSpec((1, 1, mrows, d), lambda i, j: (k, i // tiles_per_group, 0, 0))
    return pl.pallas_call(
        functools.partial(_ffn_kernel, tm=tm, rpt=rpt, tiles_per_group=tiles_per_group),
        grid=(m // tm, nj),
        in_specs=[pl.BlockSpec((tm, d), lambda i, j: (i, 0)),
                  pl.BlockSpec((1, d), lambda i, j: (0, 0)),
                  mod_spec(3), mod_spec(4), mod_spec(5),
                  pl.BlockSpec((1, d), lambda i, j: (0, 0)),
                  pl.BlockSpec((d, tf), lambda i, j: (0, j)),
                  pl.BlockSpec((d, tf), lambda i, j: (0, nj + j)),
                  pl.BlockSpec((tf, d), lambda i, j: (j, 0)),
                  pl.BlockSpec((FFN_CONV_W, tf), lambda i, j: (0, j)),
                  pl.BlockSpec((1, tf), lambda i, j: (0, j)),
                  pl.BlockSpec((1, keep, tf), lambda i, j: (i // tiles_per_group, 0, j))],
        out_specs=[pl.BlockSpec((tm, d), lambda i, j: (i, 0)),
                   pl.BlockSpec((1, keep, tf), lambda i, j: (i, 0, j))],
        out_shape=[jax.ShapeDtypeStruct((m, d), F32),
                   jax.ShapeDtypeStruct((m // tm, keep, D_FF), F32)],
        scratch_shapes=[pltpu.VMEM((tm, d), BF16), pltpu.VMEM((tm, d), F32),
                        pltpu.VMEM((pad + tm, tf), F32), pltpu.VMEM((nj, keep, tf), F32)],
        compiler_params=_cparams(("arbitrary", "arbitrary")),
        name="ffn",
    )(x, g, mod, mod, mod, gpost, w_up, w_up, w_down, cw, cb, carry_in)


def _block_diag(w):
    nblk, bw, _ = w.shape
    eye = jnp.eye(nblk, dtype=w.dtype)
    return (eye[:, None, :, None] * w[:, :, None, :]).reshape(nblk * bw, nblk * bw)


def _prep_layer_weights(p):
    w_in = p['w_in']
    row = lambda a: a.reshape(1, -1)
    pad_lanes = lambda a: jnp.pad(a.reshape(1, -1), ((0, 0), (0, 128 - a.shape[-1])))
    return {
        'w_in': w_in.astype(BF16),
        'w_tail': jnp.pad(w_in[:, D_MAIN:], ((0, 0), (0, 128 - D_TAIL))).astype(BF16),
        'g_pre_mix': row(p['g_pre_mix']), 'g_post_mix': row(p['g_post_mix']),
        'g_pre_ffn': row(p['g_pre_ffn']), 'g_post_ffn': row(p['g_post_ffn']),
        'conv_lru_w': p['conv_lru_w'], 'conv_lru_b': row(p['conv_lru_b']),
        'w_lru_r': _block_diag(p['w_lru_r']).astype(BF16), 'b_lru_r': row(p['b_lru_r']),
        'w_lru_i': _block_diag(p['w_lru_i']).astype(BF16), 'b_lru_i': row(p['b_lru_i']),
        'lru_lambda': row(p['lru_lambda']), 'g_grp_lru': row(p['g_grp_lru']), 'g_grp_sb': row(p['g_grp_sb']),
        'sb_bias': p['sb_bias'],
        'conv_gdn_w': p['conv_gdn_w'],
        'gdn_a_log': pad_lanes(p['gdn_a_log']), 'gdn_dt_bias': pad_lanes(p['gdn_dt_bias']),
        'g_gdn_norm': row(p['g_gdn_norm']),
        'w_out': p['w_out'].astype(BF16),
        'w_ffn_up': p['w_ffn_up'].astype(BF16), 'w_ffn_down': p['w_ffn_down'].astype(BF16),
        'conv_ffn_w': p['conv_ffn_w'], 'conv_ffn_b': row(p['conv_ffn_b']),
    }


def _layer_prompt(x, mod, w, *, nb, t_len):
    tm = 512
    tpg = t_len // tm
    tm_in = 1024
    proj, tail = _in_proj(x, w['g_pre_mix'], mod, w['w_in'], w['w_tail'], tm=tm_in, tiles_per_group=t_len // tm_in)
    zeros = lambda *s: jnp.zeros(s, F32)
    ya, h_last = _lru(proj, zeros(nb, CONV_W - 1, D_LRU), zeros(nb, 1, D_LRU), w['conv_lru_w'], w['conv_lru_b'],
                      w['w_lru_r'], w['b_lru_r'], w['w_lru_i'], w['b_lru_i'], w['lru_lambda'], w['g_grp_lru'],
                      nb=nb, t_len=t_len, rows=256)
    ob = _sb_prompt(proj, w['sb_bias'], nb=nb, t_len=t_len, tq=256)
    yc, s_new = _gdn(proj, tail, GDN_COL0, w['conv_gdn_w'], zeros(nb, CONV_W - 1, 3 * D_GDN),
                     w['gdn_a_log'], w['gdn_dt_bias'], zeros(nb, GDN_HEADS, GDN_HEAD_DIM, GDN_HEAD_DIM),
                     w['g_gdn_norm'], nb=nb, t_len=t_len, rows=256, chunk=128, valid_rows=None)
    x1 = _out_proj(ya, ob, yc, x, mod, w['g_grp_sb'], w['g_post_mix'], w['w_out'], tm=tm, tiles_per_group=tpg)
    x2, ffn_conv = _ffn(x1, w['g_pre_ffn'], mod, w['g_post_ffn'], w['w_ffn_up'], w['w_ffn_down'],
                        w['conv_ffn_w'], w['conv_ffn_b'], zeros(nb, FFN_CONV_W - 1, D_FF),
                        tm=tm, rpt=1, tiles_per_group=tpg, tf=512)
    ffn_conv = ffn_conv[tpg - 1::tpg]
    p3 = proj.reshape(nb, t_len, D_MAIN)
    k_new = p3[:, :, 3 * D_SB:4 * D_SB].reshape(nb, t_len, SB_HEADS, SB_HEAD_DIM)
    v_new = p3[:, :, 4 * D_SB:5 * D_SB].reshape(nb, t_len, SB_HEADS, SB_HEAD_DIM)
    lru_conv = p3[:, t_len - (CONV_W - 1):, 0:D_LRU]
    gdn_conv = p3[:, t_len - (CONV_W - 1):, GDN_COL0 * 128:GDN_COL0 * 128 + 3 * D_GDN]
    return x2, (k_new, v_new, lru_conv, h_last.reshape(nb, D_LRU), gdn_conv, s_new, ffn_conv)


def _layer_sample(x, mod_bm, mod_tm, w, layer, cache_k, cache_v, page_table, lru_buf, lru_h, gdn_buf, gdn_s,
                  ffn_buf, *, nb, t_len):
    m = nb * t_len
    proj, tail = _in_proj(x, w['g_pre_mix'], mod_bm, w['w_in'], w['w_tail'], tm=m, tiles_per_group=1)
    ya, h_last = _lru(proj, lru_buf, lru_h.reshape(nb, 1, D_LRU), w['conv_lru_w'], w['conv_lru_b'],
                      w['w_lru_r'], w['b_lru_r'], w['w_lru_i'], w['b_lru_i'], w['lru_lambda'], w['g_grp_lru'],
                      nb=nb, t_len=t_len, rows=t_len)
    p3 = proj.reshape(nb, t_len, D_MAIN)
    q = p3[:, :, 2 * D_SB:3 * D_SB]
    k_new = p3[:, :, 3 * D_SB:4 * D_SB]
    v_new = p3[:, :, 4 * D_SB:5 * D_SB]
    pad_rows = lambda a: jnp.pad(a, ((0, 0), (0, PAGE_SIZE - t_len), (0, 0)))
    bias_rows = jnp.broadcast_to(jnp.repeat(w['sb_bias'], t_len)[:, None], (SB_HEADS * t_len, PAGE_SIZE))
    as_page = lambda a: pad_rows(a).transpose(0, 2, 1)
    ob = _sb_sample(q, bias_rows, as_page(k_new), as_page(v_new), cache_k, cache_v, page_table, layer=layer)
    ob = ob.reshape(m, D_SB)
    yc, s_new = _gdn(proj, tail, GDN_COL0, w['conv_gdn_w'], gdn_buf, w['gdn_a_log'], w['gdn_dt_bias'], gdn_s,
                     w['g_gdn_norm'], nb=nb, t_len=t_len, rows=t_len, chunk=128, valid_rows=t_len)
    x1 = _out_proj(ya, ob, yc, x, mod_bm, w['g_grp_sb'], w['g_post_mix'], w['w_out'], tm=m, tiles_per_group=1)
    to_tm = lambda a: a.reshape(nb, -1, a.shape[-1]).transpose(1, 0, 2).reshape(-1, a.shape[-1])
    to_bm = lambda a: a.reshape(-1, nb, a.shape[-1]).transpose(1, 0, 2)
    x2_tm, ffn_conv_tm = _ffn(to_tm(x1), w['g_pre_ffn'], mod_tm, w['g_post_ffn'], w['w_ffn_up'], w['w_ffn_down'],
                              w['conv_ffn_w'], w['conv_ffn_b'], to_tm(ffn_buf)[None],
                              tm=m, rpt=nb, tiles_per_group=1, tf=512)
    x2 = to_bm(x2_tm).reshape(m, D_MODEL)
    ffn_conv = to_bm(ffn_conv_tm[0])
    lru_conv = p3[:, t_len - (CONV_W - 1):, 0:D_LRU]
    gdn_conv = p3[:, t_len - (CONV_W - 1):, GDN_COL0 * 128:GDN_COL0 * 128 + 3 * D_GDN]
    hd = (nb, t_len, SB_HEADS, SB_HEAD_DIM)
    return x2, (k_new.reshape(hd), v_new.reshape(hd), lru_conv, h_last.reshape(nb, D_LRU), gdn_conv, s_new, ffn_conv)


def kernel(x_prompt, x_sample, c_prompt, c_sample, cache_sb_k, cache_sb_v, page_table, state_lru_conv, state_lru_h, state_gdn_conv, state_gdn, state_ffn_conv, w_ada, b_ada, g_pre_mix, g_post_mix, g_pre_ffn, g_post_ffn, w_in, conv_lru_w, conv_lru_b, w_lru_r, b_lru_r, w_lru_i, b_lru_i, lru_lambda, g_grp_lru, g_grp_sb, sb_bias, conv_gdn_w, gdn_a_log, gdn_dt_bias, g_gdn_norm, w_out, w_ffn_up, conv_ffn_w, conv_ffn_b, w_ffn_down):
    bp, t_p, d = x_prompt.shape
    bs, t_s, _ = x_sample.shape
    depth = w_ada.shape[0]
    n_pool = cache_sb_k.shape[1]
    assert bp + bs <= MOD_ROWS and d == D_MODEL

    c_all = jnp.concatenate([c_prompt, c_sample, jnp.zeros((MOD_ROWS - bp - bs, d), F32)], axis=0)
    mod_all = _ada(c_all, w_ada, b_ada).reshape(depth, MOD_ROWS, N_MOD, d)
    as_pages = lambda c: c.transpose(0, 1, 3, 4, 2).reshape(depth, n_pool, D_SB, PAGE_SIZE)
    cache_k = as_pages(cache_sb_k)
    cache_v = as_pages(cache_sb_v)

    params = dict(w_in=w_in, g_pre_mix=g_pre_mix, g_post_mix=g_post_mix, g_pre_ffn=g_pre_ffn, g_post_ffn=g_post_ffn,
                  conv_lru_w=conv_lru_w, conv_lru_b=conv_lru_b, w_lru_r=w_lru_r, b_lru_r=b_lru_r, w_lru_i=w_lru_i,
                  b_lru_i=b_lru_i, lru_lambda=lru_lambda, g_grp_lru=g_grp_lru, g_grp_sb=g_grp_sb, sb_bias=sb_bias,
                  conv_gdn_w=conv_gdn_w, gdn_a_log=gdn_a_log, gdn_dt_bias=gdn_dt_bias, g_gdn_norm=g_gdn_norm,
                  w_out=w_out, w_ffn_up=w_ffn_up, w_ffn_down=w_ffn_down, conv_ffn_w=conv_ffn_w,
                  conv_ffn_b=conv_ffn_b)

    xp = x_prompt.reshape(bp * t_p, d)
    xs = x_sample.reshape(bs * t_s, d)
    outs_p, outs_s = [], []
    for l in range(depth):
        w = _prep_layer_weights({k: v[l] for k, v in params.items()})
        mod_l = mod_all[l]
        mod_p = mod_l[:bp].transpose(1, 0, 2)[:, :, None, :]
        mod_s = mod_l[bp:bp + bs].transpose(1, 0, 2)
        mod_s_bm = jnp.repeat(mod_s, t_s, axis=1)[:, None]
        mod_s_tm = jnp.tile(mod_s, (1, t_s, 1))[:, None]
        xp, op = _layer_prompt(xp, mod_p, w, nb=bp, t_len=t_p)
        xs, os_ = _layer_sample(xs, mod_s_bm, mod_s_tm, w, l, cache_k, cache_v, page_table,
                                state_lru_conv[l], state_lru_h[l], state_gdn_conv[l], state_gdn[l],
                                state_ffn_conv[l], nb=bs, t_len=t_s)
        outs_p.append(op)
        outs_s.append(os_)

    res = [xp.reshape(bp, t_p, d), xs.reshape(bs, t_s, d)]
    order = (0, 1, 2, 3, 4, 5, 6)
    for idx in order:
        res.append(jnp.stack([o[idx] for o in outs_p]))
        res.append(jnp.stack([o[idx] for o in outs_s]))
    y_p, y_s, kp, ks, vp, vs = res[0], res[1], res[2], res[3], res[4], res[5]
    return (y_p, y_s, kp, vp, ks, vs) + tuple(res[6:])
```

```python
import functools
import math

import jax
import jax.numpy as jnp
from jax import lax
from jax.experimental import pallas as pl
from jax.experimental.pallas import tpu as pltpu

F32 = jnp.float32
BF16 = jnp.bfloat16

D_MODEL = 2048
D_LRU = 512
LRU_BLOCKS = 8
LRU_C = 8.0
CONV_W = 4
D_SB = 512
SB_HEADS = 8
SB_HEAD_DIM = 64
D_GDN = 1024
GDN_HEADS = 8
GDN_HEAD_DIM = 128
D_FF = 5632
FFN_CONV_W = 3
N_MOD = 6
EPS = 1e-6
PAGE_SIZE = 128
D_MAIN = 2 * D_LRU + 3 * D_SB + 4 * D_GDN
D_TAIL = 2 * GDN_HEADS
GDN_COL0 = (2 * D_LRU + 3 * D_SB) // 128

VMEM_LIMIT = 56 * 1024 * 1024
MOD_ROWS = 16
SB_PAGES_PER_STEP = 8
SB_KEY_BLOCK = 256
SB_LOCKSTEP = 8
GDN_HEADS_PER_STEP = 4


def _cparams(sem):
    return pltpu.CompilerParams(dimension_semantics=sem, vmem_limit_bytes=VMEM_LIMIT)


def _dot(a, b):
    return jnp.dot(a, b, preferred_element_type=F32)


def _dot_nt(a, b):
    return lax.dot_general(a, b, (((1,), (1,)), ((), ())), preferred_element_type=F32)


def _split(a):
    hi = a.astype(BF16)
    lo = (a - hi.astype(F32)).astype(BF16)
    return hi, lo


def _mm3s(a, b):
    return _dot(a[0], b[0]) + (_dot(a[0], b[1]) + _dot(a[1], b[0]))


LOG2E = 1.4426950408889634


def _softplus(z):
    return jnp.maximum(z, 0.0) + jnp.log(1.0 + jnp.exp2(jnp.abs(z) * (-LOG2E)))


def _sigmoid(z):
    return jax.nn.sigmoid(z)


def _rms(x, g):
    return x * lax.rsqrt(jnp.mean(x * x, axis=-1, keepdims=True) + EPS) * g


def _ada_kernel(c_ref, w_ref, b_ref, o_ref):
    c = c_ref[...]
    s = c * _sigmoid(c)
    o_ref[0] = _dot(s.astype(BF16), w_ref[0].astype(BF16)) + b_ref[0]


def _ada(c_all, w_ada, b_ada):
    depth, d, n = w_ada.shape
    tn = 1024
    return pl.pallas_call(
        _ada_kernel,
        grid=(depth, n // tn),
        in_specs=[pl.BlockSpec((MOD_ROWS, d), lambda l, j: (0, 0)),
                  pl.BlockSpec((1, d, tn), lambda l, j: (l, 0, j)),
                  pl.BlockSpec((1, 1, tn), lambda l, j: (l, 0, j))],
        out_specs=pl.BlockSpec((1, MOD_ROWS, tn), lambda l, j: (l, 0, j)),
        out_shape=jax.ShapeDtypeStruct((depth, MOD_ROWS, n), F32),
        compiler_params=_cparams(("arbitrary", "arbitrary")),
        name="ada",
    )(c_all, w_ada, b_ada.reshape(depth, 1, n))


def _in_proj_kernel(x_ref, g_ref, sh_ref, sc_ref, w_ref, wt_ref, o_ref, ot_ref, h_scr):
    j = pl.program_id(1)

    @pl.when(j == 0)
    def _():
        h = _rms(x_ref[...], g_ref[...]) * (1.0 + sc_ref[0, 0]) + sh_ref[0, 0]
        hb = h.astype(BF16)
        h_scr[...] = hb
        ot_ref[...] = _dot(hb, wt_ref[...])

    o_ref[...] = _dot(h_scr[...], w_ref[...])


def _in_proj(x, g, mod, w_all, w_tail, *, layer, tm, tiles_per_group):
    m, d = x.shape
    n = D_MAIN
    tn = 512
    mrows = mod.shape[2]
    mod_spec = lambda k: pl.BlockSpec((1, 1, mrows, d), lambda i, j: (k, i // tiles_per_group, 0, 0))
    return pl.pallas_call(
        _in_proj_kernel,
        grid=(m // tm, n // tn),
        in_specs=[pl.BlockSpec((tm, d), lambda i, j: (i, 0)),
                  pl.BlockSpec((1, d), lambda i, j: (0, 0)),
                  mod_spec(0), mod_spec(1),
                  pl.BlockSpec((None, d, tn), lambda i, j: (layer, 0, j)),
                  pl.BlockSpec((None, d, 128), lambda i, j: (layer, 0, 0))],
        out_specs=[pl.BlockSpec((tm, tn), lambda i, j: (i, j)),
                   pl.BlockSpec((tm, 128), lambda i, j: (i, 0))],
        out_shape=[jax.ShapeDtypeStruct((m, n), F32),
                   jax.ShapeDtypeStruct((m, 128), F32)],
        scratch_shapes=[pltpu.VMEM((tm, d), BF16)],
        compiler_params=_cparams(("arbitrary", "arbitrary")),
        name="in_proj",
    )(x, g, mod, mod, w_all, w_tail)


def _lru_kernel(ax_ref, ag_ref, buf_ref, h0_ref, cw_ref, cb_ref, wr_ref, br_ref, wi_ref, bi_ref, lam_ref, gg_ref,
                y_ref, hl_ref, xp_scr, h_scr, *, rows):
    t = pl.program_id(1)

    @pl.when(t == 0)
    def _():
        xp_scr[5:8, :] = buf_ref[0]
        h_scr[...] = h0_ref[0]

    x = ax_ref[...]
    xp_scr[8:8 + rows, :] = x
    cw = cw_ref[...]
    xc = (xp_scr[5:5 + rows, :] * cw[0:1] + xp_scr[6:6 + rows, :] * cw[1:2]
          + xp_scr[7:7 + rows, :] * cw[2:3] + x * cw[3:4]) + cb_ref[...]
    xp_scr[5:8, :] = xp_scr[5 + rows:8 + rows, :]

    xb = xc.astype(BF16)
    r = _sigmoid(_dot(xb, wr_ref[...]) + br_ref[...])
    ig = _sigmoid(_dot(xb, wi_ref[...]) + bi_ref[...])
    log_a = (LRU_C * r) * (-_softplus(-lam_ref[...]))
    a = jnp.exp(log_a)
    u = jnp.sqrt(1.0 - a * a) * (ig * xc)

    ridx = lax.broadcasted_iota(jnp.int32, a.shape, 0)
    s = 1
    while s < rows:
        a_sh = pltpu.roll(a, s, 0)
        u_sh = pltpu.roll(u, s, 0)
        m = ridx >= s
        u = jnp.where(m, a * u_sh + u, u)
        a = jnp.where(m, a * a_sh, a)
        s *= 2
    h = a * h_scr[...] + u
    h_last = h[rows - 1:rows, :]
    h_scr[...] = h_last
    hl_ref[0] = h_last

    ag = ag_ref[...]
    gelu = 0.5 * ag * (1.0 + jnp.tanh(math.sqrt(2.0 / math.pi) * (ag + 0.044715 * (ag * ag * ag))))
    y_ref[...] = _rms(h * gelu, gg_ref[...])


def _lru(proj, buf, h0, cw, cb, wr, br, wi, bi, lam, gg, *, nb, t_len, rows):
    m = proj.shape[0]
    nt = t_len // rows
    row_map = lambda c: (lambda b, t: (b * nt + t, c))
    vec = lambda: pl.BlockSpec((1, D_LRU), lambda b, t: (0, 0))
    return pl.pallas_call(
        functools.partial(_lru_kernel, rows=rows),
        grid=(nb, nt),
        in_specs=[pl.BlockSpec((rows, D_LRU), row_map(0)),
                  pl.BlockSpec((rows, D_LRU), row_map(1)),
                  pl.BlockSpec((1, CONV_W - 1, D_LRU), lambda b, t: (b, 0, 0)),
                  pl.BlockSpec((1, 1, D_LRU), lambda b, t: (b, 0, 0)),
                  pl.BlockSpec((CONV_W, D_LRU), lambda b, t: (0, 0)),
                  vec(),
                  pl.BlockSpec((D_LRU, D_LRU), lambda b, t: (0, 0)), vec(),
                  pl.BlockSpec((D_LRU, D_LRU), lambda b, t: (0, 0)), vec(),
                  vec(), vec()],
        out_specs=[pl.BlockSpec((rows, D_LRU), lambda b, t: (b * nt + t, 0)),
                   pl.BlockSpec((1, 1, D_LRU), lambda b, t: (b, 0, 0))],
        out_shape=[jax.ShapeDtypeStruct((m, D_LRU), F32),
                   jax.ShapeDtypeStruct((nb, 1, D_LRU), F32)],
        scratch_shapes=[pltpu.VMEM((rows + 8, D_LRU), F32), pltpu.VMEM((1, D_LRU), F32)],
        compiler_params=_cparams(("arbitrary", "arbitrary")),
        name="lru",
    )(proj, proj, buf, h0, cw, cb, wr, br, wi, bi, lam, gg)


def _neg_tri2(n):
    row = lax.broadcasted_iota(jnp.int32, (2 * n, n), 0) & (n - 1)
    col = lax.broadcasted_iota(jnp.int32, (2 * n, n), 1)
    return jnp.where(row >= col, -1.0, 0.0).astype(BF16)


def _sb_stages(zs, mask, ntri2, carries):
    ss = [_softplus(z) for z in zs]
    if mask is not None:
        ss = [jnp.where(mask, s, 0.0) for s in ss]
    parts = [_split(s) for s in ss]
    after = [_dot(jnp.concatenate([hi, lo], axis=1), ntri2) for hi, lo in parts]
    ws = [jnp.exp(z + (a + c)) for z, a, c in zip(zs, after, carries)]
    if mask is not None:
        ws = [jnp.where(mask, w, 0.0) for w in ws]
    return ws, [jnp.sum(s, axis=1, keepdims=True) for s in ss]


def _sb_prompt_kernel(bias_ref, q_ref, k_ref, v_ref, o_ref, car_scr, *, tq):
    qi = pl.program_id(1)
    scale = SB_HEAD_DIM ** -0.5
    ntri2 = _neg_tri2(tq)
    row = lax.broadcasted_iota(jnp.int32, (tq, tq), 0)
    col = lax.broadcasted_iota(jnp.int32, (tq, tq), 1)
    causal = col < row
    lo_half = lax.broadcasted_iota(jnp.int32, (tq, 128), 1) < SB_HEAD_DIM
    npair = SB_HEADS // 2

    qm = []
    for pr in range(npair):
        qp = q_ref[:, pr * 128:(pr + 1) * 128] * scale
        qm.append((jnp.where(lo_half, qp, 0.0).astype(BF16), jnp.where(lo_half, 0.0, qp).astype(BF16)))
    o_ref[...] = jnp.zeros_like(o_ref)
    car_scr[...] = jnp.zeros_like(car_scr)

    def blocks(kb, mask):
        start = pl.multiple_of(kb * tq, tq)
        for grp in range(SB_HEADS // SB_LOCKSTEP):
            hs = list(range(grp * SB_LOCKSTEP, (grp + 1) * SB_LOCKSTEP))
            kp, vm = {}, {}
            for pr in sorted({h // 2 for h in hs}):
                cols = slice(pr * 128, (pr + 1) * 128)
                kp[pr] = k_ref[pl.ds(start, tq), cols].astype(BF16)
                vp = v_ref[pl.ds(start, tq), cols]
                vm[pr] = (jnp.where(lo_half, vp, 0.0).astype(BF16), jnp.where(lo_half, 0.0, vp).astype(BF16))
            z = [_dot_nt(qm[h // 2][h % 2], kp[h // 2]) + bias_ref[h] for h in hs]
            car = [car_scr[h] for h in hs]
            w, tot = _sb_stages(z, mask, ntri2, car)
            for h, c, t_ in zip(hs, car, tot):
                car_scr[h] = c - t_
            o = [_dot(x.astype(BF16), vm[h // 2][h % 2]) for x, h in zip(w, hs)]
            c0 = (hs[0] // 2) * 128
            o_ref[:, c0:c0 + 64 * len(hs)] += jnp.concatenate(
                [o[i] + o[i + 1] for i in range(0, len(hs), 2)], axis=1)

    blocks(qi, causal)

    def body(it, c):
        blocks(qi - 1 - it, None)
        return c

    lax.fori_loop(0, qi, body, 0)


def _sb_prompt(proj, bias, *, nb, t_len, tq):
    m = proj.shape[0]
    nq = t_len // tq
    qcol, kcol, vcol = (2 * D_LRU) // D_SB, (2 * D_LRU) // D_SB + 1, (2 * D_LRU) // D_SB + 2
    return pl.pallas_call(
        functools.partial(_sb_prompt_kernel, tq=tq),
        grid=(nb, nq),
        in_specs=[pl.BlockSpec(memory_space=pltpu.SMEM),
                  pl.BlockSpec((tq, D_SB), lambda b, i: (b * nq + i, qcol)),
                  pl.BlockSpec((t_len, D_SB), lambda b, i: (b, kcol)),
                  pl.BlockSpec((t_len, D_SB), lambda b, i: (b, vcol))],
        out_specs=pl.BlockSpec((tq, D_SB), lambda b, i: (b * nq + i, 0)),
        out_shape=jax.ShapeDtypeStruct((m, D_SB), F32),
        scratch_shapes=[pltpu.VMEM((SB_HEADS, tq, 1), F32)],
        compiler_params=_cparams(("arbitrary", "arbitrary")),
        name="sb_prompt",
    )(bias, proj, proj, proj)


def _sb_sample_kernel(pt_ref, q_ref, bias_ref, kn_ref, vn_ref, *rest, t_new, group):
    del pt_ref
    kc_refs, vc_refs = rest[:group], rest[group:2 * group]
    o_ref, acc_scr, car_scr = rest[2 * group:]
    p = pl.program_id(1)
    rows = SB_HEADS * t_new
    scale = SB_HEAD_DIM ** -0.5
    q = q_ref[0] * scale
    q_rep = jnp.concatenate([q] * SB_HEADS, axis=0)
    rh = lax.broadcasted_iota(jnp.int32, (rows, D_SB), 0) >> int(math.log2(t_new))
    ch = lax.broadcasted_iota(jnp.int32, (rows, D_SB), 1) >> int(math.log2(SB_HEAD_DIM))
    own = rh == ch
    qbd = jnp.where(own, q_rep, 0.0).astype(BF16)
    bias = bias_ref[...]

    def keys(kt, vt, width, mask):
        n = kt.shape[1]
        nblk = n // width
        z = _dot(qbd, kt) + jnp.concatenate([bias] * (n // PAGE_SIZE), axis=1)
        s = _softplus(z)
        if mask is not None:
            s = jnp.where(mask, s, 0.0)
        sk = [s[:, k * width:(k + 1) * width] for k in range(nblk)]
        tot = [jnp.sum(x, axis=1, keepdims=True) for x in sk]
        hi, lo = _split(jnp.concatenate(sk, axis=0) if nblk > 1 else sk[0])
        aft = _dot(jnp.concatenate([hi, lo], axis=1), _neg_tri2(width))
        car = car_scr[...]
        ws = [None] * nblk
        for k in reversed(range(nblk)):
            ws[k] = jnp.exp(z[:, k * width:(k + 1) * width] + (aft[k * rows:(k + 1) * rows] + car))
            car = car - tot[k]
        w = jnp.concatenate(ws, axis=1) if nblk > 1 else ws[0]
        if mask is not None:
            w = jnp.where(mask, w, 0.0)
        car_scr[...] = car
        acc_scr[...] += _dot_nt(w.astype(BF16), vt)

    @pl.when(p == 0)
    def _():
        acc_scr[...] = jnp.zeros_like(acc_scr)
        car_scr[...] = jnp.zeros_like(car_scr)
        qpos = lax.broadcasted_iota(jnp.int32, (rows, PAGE_SIZE), 0) & (t_new - 1)
        kpos = lax.broadcasted_iota(jnp.int32, (rows, PAGE_SIZE), 1)
        keys(kn_ref[0].astype(BF16), vn_ref[0].astype(BF16), PAGE_SIZE, kpos < qpos)

    keys(jnp.concatenate([r[...].astype(BF16) for r in kc_refs], axis=1),
         jnp.concatenate([r[...].astype(BF16) for r in vc_refs], axis=1), SB_KEY_BLOCK, None)

    @pl.when(p == pl.num_programs(1) - 1)
    def _():
        acc = jnp.where(own, acc_scr[...], 0.0)
        out = acc[0:t_new]
        for h in range(1, SB_HEADS):
            out = out + acc[h * t_new:(h + 1) * t_new]
        o_ref[0] = out


def _sb_sample(q, bias_rows, k_new, v_new, cache_k, cache_v, page_table, *, layer):
    nb, t_new, _ = q.shape
    n_pages = page_table.shape[1]
    rows = SB_HEADS * t_new
    group = SB_PAGES_PER_STEP
    assert n_pages % group == 0

    def cache_spec(g):
        return pl.BlockSpec((None, None, D_SB, PAGE_SIZE),
                            lambda b, p, pt: (layer, pt[b, n_pages - group * (p + 1) + g], 0, 0))

    grid_spec = pltpu.PrefetchScalarGridSpec(
        num_scalar_prefetch=1,
        grid=(nb, n_pages // group),
        in_specs=[pl.BlockSpec((1, t_new, D_SB), lambda b, p, pt: (b, 0, 0)),
                  pl.BlockSpec((rows, PAGE_SIZE), lambda b, p, pt: (0, 0)),
                  pl.BlockSpec((1, D_SB, PAGE_SIZE), lambda b, p, pt: (b, 0, 0)),
                  pl.BlockSpec((1, D_SB, PAGE_SIZE), lambda b, p, pt: (b, 0, 0))]
                 + [cache_spec(g) for g in range(group)] * 2,
        out_specs=pl.BlockSpec((1, t_new, D_SB), lambda b, p, pt: (b, 0, 0)),
        scratch_shapes=[pltpu.VMEM((rows, D_SB), F32), pltpu.VMEM((rows, 1), F32)],
    )
    return pl.pallas_call(
        functools.partial(_sb_sample_kernel, t_new=t_new, group=group),
        grid_spec=grid_spec,
        out_shape=jax.ShapeDtypeStruct((nb, t_new, D_SB), F32),
        compiler_params=_cparams(("arbitrary", "arbitrary")),
        name="sb_sample",
    )(page_table, q, bias_rows, k_new, v_new, *([cache_k] * group), *([cache_v] * group))


def _unit_lower_inverses(lmats, c, support):
    ii = lax.broadcasted_iota(jnp.int32, (c, c), 0)
    jj = lax.broadcasted_iota(jnp.int32, (c, c), 1)
    eye = jnp.where(ii == jj, 1.0, 0.0)
    base = min(16, c)
    shift = int(math.log2(base))
    same_base = (ii >> shift) == (jj >> shift)
    pws = [jnp.where(same_base, l, 0.0) for l in lmats]
    xs = [eye - p for p in pws]
    ps = [_split(p) for p in pws]
    n = 2
    while n < min(base, support):
        ps = [_split(_mm3s(p, p)) for p in ps]
        xs = [x + _mm3s(_split(x), p) for x, p in zip(xs, ps)]
        n *= 2
    s = base
    while s < min(c, support):
        sh = int(math.log2(s))
        below = ((ii >> (sh + 1)) == (jj >> (sh + 1))) & ((ii >> sh) != (jj >> sh))
        xsp = [_split(x) for x in xs]
        ys = [_mm3s(x, _split(jnp.where(below, l, 0.0))) for x, l in zip(xsp, lmats)]
        xs = [x - _mm3s(_split(y), xp) for x, y, xp in zip(xs, ys, xsp)]
        s *= 2
    return xs


def _gdn_kernel(q_ref, k_ref, v_ref, z_ref, tail_ref, cwq_ref, cwk_ref, cwv_ref, bq_ref, bk_ref, bv_ref,
                alog_ref, dtb_ref, s0_ref, gn_ref, y_ref, s_ref, xq_scr, xk_scr, xv_scr,
                *, rows, chunk, valid_rows, hg):
    t = pl.program_id(2)
    dk = GDN_HEAD_DIM

    @pl.when(t == 0)
    def _():
        xq_scr[5:8, :] = bq_ref[0]
        xk_scr[5:8, :] = bk_ref[0]
        xv_scr[5:8, :] = bv_ref[0]
        s_ref[0] = s0_ref[0]

    def conv_silu(x_ref, scr, cw_ref):
        x = x_ref[...]
        scr[8:8 + rows, :] = x
        cw = cw_ref[...]
        y = (scr[5:5 + rows, :] * cw[0:1] + scr[6:6 + rows, :] * cw[1:2]
             + scr[7:7 + rows, :] * cw[2:3] + x * cw[3:4])
        scr[5:8, :] = scr[5 + rows:8 + rows, :]
        return y * _sigmoid(y)

    q_all = conv_silu(q_ref, xq_scr, cwq_ref)
    k_all = conv_silu(k_ref, xk_scr, cwk_ref)
    v_all = conv_silu(v_ref, xv_scr, cwv_ref)

    tail = tail_ref[...]
    crows = -(-rows // chunk) * chunk
    if crows > rows:
        assert valid_rows is not None and valid_rows <= rows
        pad0 = lambda a: jnp.concatenate([a, jnp.zeros((crows - rows, a.shape[1]), F32)], axis=0)
        q_all, k_all, v_all, tail = pad0(q_all), pad0(k_all), pad0(v_all), pad0(tail)
    lane = lax.broadcasted_iota(jnp.int32, tail.shape, 1)
    lane1 = lax.broadcasted_iota(jnp.int32, (1, 128), 1)
    ii = lax.broadcasted_iota(jnp.int32, (chunk, chunk), 0)
    jj = lax.broadcasted_iota(jnp.int32, (chunk, chunk), 1)
    incl = ii >= jj
    strict = ii > jj
    ltri = jnp.where(incl, 1.0, 0.0).astype(BF16)
    support = chunk
    if valid_rows is not None:
        ok = lax.broadcasted_iota(jnp.int32, (crows, 1), 0) + t * rows < valid_rows
        support = min(chunk, 1 << max(valid_rows - 1, 0).bit_length())

    nchunk = crows // chunk
    heads = list(range(hg))
    chains = [(hh, c) for hh in heads for c in range(nchunk)]
    qn, kn, beta, g = [], [], [], []
    for hh in heads:
        h = pl.program_id(1) * hg + hh
        cs = slice(hh * dk, (hh + 1) * dk)
        q, k = q_all[:, cs], k_all[:, cs]
        qn.append(q * lax.rsqrt(jnp.sum(q * q, axis=-1, keepdims=True) + EPS) * (dk ** -0.5))
        kn.append(k * lax.rsqrt(jnp.sum(k * k, axis=-1, keepdims=True) + EPS))
        b_col = jnp.sum(jnp.where(lane == h, tail, 0.0), axis=1, keepdims=True)
        a_col = jnp.sum(jnp.where(lane == GDN_HEADS + h, tail, 0.0), axis=1, keepdims=True)
        a_log = jnp.sum(jnp.where(lane1 == h, alog_ref[...], 0.0), axis=1, keepdims=True)
        dt_b = jnp.sum(jnp.where(lane1 == h, dtb_ref[...], 0.0), axis=1, keepdims=True)
        b_h = _sigmoid(b_col)
        g_h = -jnp.exp(a_log) * _softplus(a_col + dt_b)
        if valid_rows is not None:
            b_h = jnp.where(ok, b_h, 0.0)
            g_h = jnp.where(ok, g_h, 0.0)
        beta.append(b_h)
        g.append(g_h)

    def per_chain(per_head, lanes=False):
        out = []
        for hh, c in chains:
            a = per_head[hh]
            if lanes:
                a = a[:, hh * dk:(hh + 1) * dk]
            out.append(a[c * chunk:(c + 1) * chunk])
        return out

    g_c, beta_c, q_c, k_c = per_chain(g), per_chain(beta), per_chain(qn), per_chain(kn)
    v_c = per_chain([v_all] * hg, lanes=True)
    gcum = [_mm_exact_rhs_lhs(ltri, jnp.broadcast_to(x, (chunk, dk))) for x in g_c]
    gdiff = [_mm_exact_rhs_lhs(ltri, jnp.where(strict, jnp.broadcast_to(x, (chunk, chunk)), 0.0)) for x in g_c]
    decay = [jnp.where(incl, jnp.exp(x), 0.0) for x in gdiff]
    kb = [a * b for a, b in zip(k_c, beta_c)]
    k_bf = [a.astype(BF16) for a in k_c]
    lmat = [jnp.where(strict, _dot_nt(a.astype(BF16), kk) * d, 0.0) for a, kk, d in zip(kb, k_bf, decay)]
    qk = [(_dot_nt(a.astype(BF16), kk) * d).astype(BF16) for a, kk, d in zip(q_c, k_bf, decay)]
    tinv = [x.astype(BF16) for x in _unit_lower_inverses(lmat, chunk, support)]
    egc = [jnp.exp(x) for x in gcum]
    u = [_dot(ti, (a * b).astype(BF16)) for ti, a, b in zip(tinv, v_c, beta_c)]
    w = [_dot(ti, (a * e).astype(BF16)).astype(BF16) for ti, a, e in zip(tinv, kb, egc)]
    g_last = [x[chunk - 1:chunk, :] for x in gcum]
    qg = [(a * e).astype(BF16) for a, e in zip(q_c, egc)]
    kd_t = [(a * jnp.exp(gl - gc)).T.astype(BF16) for a, gl, gc in zip(k_c, g_last, gcum)]
    eg = [jnp.exp(gl) for gl in g_last]

    gn = gn_ref[...]
    s = [s_ref[0, hh] for hh in heads]
    outs = [[] for _ in heads]
    for c in range(nchunk):
        idx = [hh * nchunk + c for hh in heads]
        sb = [x.astype(BF16) for x in s]
        v_new = [u[i] - _dot(w[i], sb[hh]) for hh, i in zip(heads, idx)]
        o_state = [_dot(qg[i], sb[hh]) for hh, i in zip(heads, idx)]
        vb = [x.astype(BF16) for x in v_new]
        o = [o_state[hh] + _dot(qk[i], vb[hh]) for hh, i in zip(heads, idx)]
        s = [s[hh] * eg[i] + _dot(kd_t[i], vb[hh]) for hh, i in zip(heads, idx)]
        for hh in heads:
            outs[hh].append(_rms(o[hh], gn))
    ys = [jnp.concatenate(o, axis=0) if nchunk > 1 else o[0] for o in outs]
    z_all = z_ref[...]
    y_ref[...] = jnp.concatenate(ys, axis=1)[:rows] * (z_all * _sigmoid(z_all))
    s_ref[0] = jnp.stack(s, axis=0)


def _mm_exact_rhs_lhs(a_bf16, b):
    b_hi, b_lo = _split(b)
    return _dot(a_bf16, b_hi) + _dot(a_bf16, b_lo)


def _gdn(arr, tail, col0, cw, buf, a_log, dt_bias, s0, gn, *, nb, t_len, rows, chunk, valid_rows):
    m = arr.shape[0]
    nt = t_len // rows
    hd = GDN_HEAD_DIM
    nh = GDN_HEADS
    hg = GDN_HEADS_PER_STEP
    wd = hg * hd
    assert nh % hg == 0 and col0 % hg == 0
    col = lambda base: pl.BlockSpec((rows, wd), lambda b, h, t: (b * nt + t, (col0 + base) // hg + h))
    cwspec = lambda base: pl.BlockSpec((CONV_W, wd), lambda b, h, t: (0, base // hg + h))
    bufspec = lambda base: pl.BlockSpec((1, CONV_W - 1, wd), lambda b, h, t: (b, 0, base // hg + h))
    vec = lambda: pl.BlockSpec((1, 128), lambda b, h, t: (0, 0))
    sspec = lambda: pl.BlockSpec((1, hg, hd, hd), lambda b, h, t: (b, h, 0, 0))
    return pl.pallas_call(
        functools.partial(_gdn_kernel, rows=rows, chunk=chunk, valid_rows=valid_rows, hg=hg),
        grid=(nb, nh // hg, nt),
        in_specs=[col(0), col(nh), col(2 * nh), col(3 * nh),
                  pl.BlockSpec((rows, 128), lambda b, h, t: (b * nt + t, 0)),
                  cwspec(0), cwspec(nh), cwspec(2 * nh),
                  bufspec(0), bufspec(nh), bufspec(2 * nh),
                  vec(), vec(), sspec(), vec()],
        out_specs=[pl.BlockSpec((rows, wd), lambda b, h, t: (b * nt + t, h)), sspec()],
        out_shape=[jax.ShapeDtypeStruct((m, D_GDN), F32),
                   jax.ShapeDtypeStruct((nb, nh, hd, hd), F32)],
        scratch_shapes=[pltpu.VMEM((rows + 8, wd), F32)] * 3,
        compiler_params=_cparams(("arbitrary", "arbitrary", "arbitrary")),
        name="gdn",
    )(arr, arr, arr, arr, tail, cw, cw, cw, buf, buf, buf, a_log, dt_bias, s0, gn)


def _out_proj_kernel(ya_ref, ob_ref, yc_ref, x_ref, gate_ref, gsb_ref, gpost_ref, w_ref, o_ref):
    yb = _rms(ob_ref[...], gsb_ref[...])
    mix = (_dot(ya_ref[...].astype(BF16), w_ref[0:D_LRU, :])
           + _dot(yb.astype(BF16), w_ref[D_LRU:D_LRU + D_SB, :])
           + _dot(yc_ref[...].astype(BF16), w_ref[D_LRU + D_SB:, :]))
    o_ref[...] = x_ref[...] + gate_ref[0, 0] * _rms(mix, gpost_ref[...])


def _out_proj(ya, ob, yc, x, mod, gsb, gpost, w_out, *, layer, tm, tiles_per_group):
    m, d = x.shape
    mrows = mod.shape[2]
    return pl.pallas_call(
        _out_proj_kernel,
        grid=(m // tm,),
        in_specs=[pl.BlockSpec((tm, D_LRU), lambda i: (i, 0)),
                  pl.BlockSpec((tm, D_SB), lambda i: (i, 0)),
                  pl.BlockSpec((tm, D_GDN), lambda i: (i, 0)),
                  pl.BlockSpec((tm, d), lambda i: (i, 0)),
                  pl.BlockSpec((1, 1, mrows, d), lambda i: (2, i // tiles_per_group, 0, 0)),
                  pl.BlockSpec((1, D_SB), lambda i: (0, 0)),
                  pl.BlockSpec((1, d), lambda i: (0, 0)),
                  pl.BlockSpec((None, d, d), lambda i: (layer, 0, 0))],
        out_specs=pl.BlockSpec((tm, d), lambda i: (i, 0)),
        out_shape=jax.ShapeDtypeStruct((m, d), F32),
        compiler_params=_cparams(("arbitrary",)),
        name="out_proj",
    )(ya, ob, yc, x, mod, gsb, gpost, w_out)


def _ffn_kernel(x_ref, g_ref, sh_ref, sc_ref, gate_ref, gpost_ref, wg_ref, wv_ref, wd_ref, cw_ref, cb_ref, cin_ref,
                o_ref, cout_ref, h_scr, acc_scr, gbuf_scr, carry_scr, *, tm, rpt, tiles_per_group):
    i = pl.program_id(0)
    j = pl.program_id(1)
    pad = gbuf_scr.shape[0] - tm
    keep = (FFN_CONV_W - 1) * rpt

    @pl.when(j == 0)
    def _():
        h = _rms(x_ref[...], g_ref[...]) * (1.0 + sc_ref[0, 0]) + sh_ref[0, 0]
        h_scr[...] = h.astype(BF16)
        acc_scr[...] = jnp.zeros_like(acc_scr)

    first = (i % tiles_per_group) == 0

    @pl.when(first)
    def _():
        gbuf_scr[pad - keep:pad, :] = cin_ref[0]

    @pl.when(jnp.logical_not(first))
    def _():
        gbuf_scr[pad - keep:pad, :] = carry_scr[j]

    hb = h_scr[...]
    gate = _dot(hb, wg_ref[...])
    val = _dot(hb, wv_ref[...])
    gbuf_scr[pad:pad + tm, :] = gate
    cw = cw_ref[...]
    gc = (gbuf_scr[pad - 2 * rpt:pad - 2 * rpt + tm, :] * cw[0:1]
          + gbuf_scr[pad - rpt:pad - rpt + tm, :] * cw[1:2] + gate * cw[2:3]) + cb_ref[...]
    last = gbuf_scr[pad + tm - keep:pad + tm, :]
    carry_scr[j] = last
    cout_ref[0] = last
    f = (gc * _sigmoid(gc)) * val
    acc_scr[...] += _dot(f.astype(BF16), wd_ref[...])

    @pl.when(j == pl.num_programs(1) - 1)
    def _():
        o_ref[...] = x_ref[...] + gate_ref[0, 0] * _rms(acc_scr[...], gpost_ref[...])


def _ffn(x, g, mod, gpost, w_up, w_down, cw, cb, carry_in, *, layer, tm, rpt, tiles_per_group, tf):
    m, d = x.shape
    nj = D_FF // tf
    mrows = mod.shape[2]
    keep = (FFN_CONV_W - 1) * rpt
    pad = -(-keep // 8) * 8
    mod_spec = lambda k: pl.BlockSpec((1, 1, mrows, d), lambda i, j: (k, i // tiles_per_group, 0, 0))
    return pl.pallas_call(
        functools.partial(_ffn_kernel, tm=tm, rpt=rpt, tiles_per_group=tiles_per_group),
        grid=(m // tm, nj),
        in_specs=[pl.BlockSpec((tm, d), lambda i, j: (i, 0)),
                  pl.BlockSpec((1, d), lambda i, j: (0, 0)),
                  mod_spec(3), mod_spec(4), mod_spec(5),
                  pl.BlockSpec((1, d), lambda i, j: (0, 0)),
                  pl.BlockSpec((None, d, tf), lambda i, j: (layer, 0, j)),
                  pl.BlockSpec((None, d, tf), lambda i, j: (layer, 0, nj + j)),
                  pl.BlockSpec((None, tf, d), lambda i, j: (layer, j, 0)),
                  pl.BlockSpec((FFN_CONV_W, tf), lambda i, j: (0, j)),
                  pl.BlockSpec((1, tf), lambda i, j: (0, j)),
                  pl.BlockSpec((1, keep, tf), lambda i, j: (i // tiles_per_group, 0, j))],
        out_specs=[pl.BlockSpec((tm, d), lambda i, j: (i, 0)),
                   pl.BlockSpec((1, keep, tf), lambda i, j: (i, 0, j))],
        out_shape=[jax.ShapeDtypeStruct((m, d), F32),
                   jax.ShapeDtypeStruct((m // tm, keep, D_FF), F32)],
        scratch_shapes=[pltpu.VMEM((tm, d), BF16), pltpu.VMEM((tm, d), F32),
                        pltpu.VMEM((pad + tm, tf), F32), pltpu.VMEM((nj, keep, tf), F32)],
        compiler_params=_cparams(("arbitrary", "arbitrary")),
        name="ffn",
    )(x, g, mod, mod, mod, gpost, w_up, w_up, w_down, cw, cb, carry_in)


def _block_diag(w):
    nblk, bw, _ = w.shape
    eye = jnp.eye(nblk, dtype=w.dtype)
    return (eye[:, None, :, None] * w[:, :, None, :]).reshape(nblk * bw, nblk * bw)


def _prep_layer_weights(p):
    row = lambda a: a.reshape(1, -1)
    pad_lanes = lambda a: jnp.pad(a.reshape(1, -1), ((0, 0), (0, 128 - a.shape[-1])))
    return {
        'g_pre_mix': row(p['g_pre_mix']), 'g_post_mix': row(p['g_post_mix']),
        'g_pre_ffn': row(p['g_pre_ffn']), 'g_post_ffn': row(p['g_post_ffn']),
        'conv_lru_w': p['conv_lru_w'], 'conv_lru_b': row(p['conv_lru_b']),
        'w_lru_r': _block_diag(p['w_lru_r']).astype(BF16), 'b_lru_r': row(p['b_lru_r']),
        'w_lru_i': _block_diag(p['w_lru_i']).astype(BF16), 'b_lru_i': row(p['b_lru_i']),
        'lru_lambda': row(p['lru_lambda']), 'g_grp_lru': row(p['g_grp_lru']), 'g_grp_sb': row(p['g_grp_sb']),
        'sb_bias': p['sb_bias'],
        'conv_gdn_w': p['conv_gdn_w'],
        'gdn_a_log': pad_lanes(p['gdn_a_log']), 'gdn_dt_bias': pad_lanes(p['gdn_dt_bias']),
        'g_gdn_norm': row(p['g_gdn_norm']),
        'conv_ffn_w': p['conv_ffn_w'], 'conv_ffn_b': row(p['conv_ffn_b']),
    }


def _layer_prompt(x, mod, w, *, nb, t_len):
    tm = 512
    tpg = t_len // tm
    tm_in = 1024
    proj, tail = _in_proj(x, w['g_pre_mix'], mod, w['w_in'], w['w_tail'], layer=w['layer'], tm=tm_in,
                          tiles_per_group=t_len // tm_in)
    zeros = lambda *s: jnp.zeros(s, F32)
    ya, h_last = _lru(proj, zeros(nb, CONV_W - 1, D_LRU), zeros(nb, 1, D_LRU), w['conv_lru_w'], w['conv_lru_b'],
                      w['w_lru_r'], w['b_lru_r'], w['w_lru_i'], w['b_lru_i'], w['lru_lambda'], w['g_grp_lru'],
                      nb=nb, t_len=t_len, rows=256)
    ob = _sb_prompt(proj, w['sb_bias'], nb=nb, t_len=t_len, tq=256)
    yc, s_new = _gdn(proj, tail, GDN_COL0, w['conv_gdn_w'], zeros(nb, CONV_W - 1, 3 * D_GDN),
                     w['gdn_a_log'], w['gdn_dt_bias'], zeros(nb, GDN_HEADS, GDN_HEAD_DIM, GDN_HEAD_DIM),
                     w['g_gdn_norm'], nb=nb, t_len=t_len, rows=256, chunk=128, valid_rows=None)
    x1 = _out_proj(ya, ob, yc, x, mod, w['g_grp_sb'], w['g_post_mix'], w['w_out'], layer=w['layer'], tm=tm,
                   tiles_per_group=tpg)
    x2, ffn_conv = _ffn(x1, w['g_pre_ffn'], mod, w['g_post_ffn'], w['w_ffn_up'], w['w_ffn_down'],
                        w['conv_ffn_w'], w['conv_ffn_b'], zeros(nb, FFN_CONV_W - 1, D_FF),
                        layer=w['layer'], tm=tm, rpt=1, tiles_per_group=tpg, tf=512)
    ffn_conv = ffn_conv[tpg - 1::tpg]
    p3 = proj.reshape(nb, t_len, D_MAIN)
    k_new = p3[:, :, 3 * D_SB:4 * D_SB].reshape(nb, t_len, SB_HEADS, SB_HEAD_DIM)
    v_new = p3[:, :, 4 * D_SB:5 * D_SB].reshape(nb, t_len, SB_HEADS, SB_HEAD_DIM)
    lru_conv = p3[:, t_len - (CONV_W - 1):, 0:D_LRU]
    gdn_conv = p3[:, t_len - (CONV_W - 1):, GDN_COL0 * 128:GDN_COL0 * 128 + 3 * D_GDN]
    return x2, (k_new, v_new, lru_conv, h_last.reshape(nb, D_LRU), gdn_conv, s_new, ffn_conv)


def _layer_sample(x, mod_bm, mod_tm, w, layer, cache_k, cache_v, page_table, lru_buf, lru_h, gdn_buf, gdn_s,
                  ffn_buf, *, nb, t_len):
    m = nb * t_len
    proj, tail = _in_proj(x, w['g_pre_mix'], mod_bm, w['w_in'], w['w_tail'], layer=layer, tm=m,
                          tiles_per_group=1)
    ya, h_last = _lru(proj, lru_buf, lru_h.reshape(nb, 1, D_LRU), w['conv_lru_w'], w['conv_lru_b'],
                      w['w_lru_r'], w['b_lru_r'], w['w_lru_i'], w['b_lru_i'], w['lru_lambda'], w['g_grp_lru'],
                      nb=nb, t_len=t_len, rows=t_len)
    p3 = proj.reshape(nb, t_len, D_MAIN)
    q = p3[:, :, 2 * D_SB:3 * D_SB]
    k_new = p3[:, :, 3 * D_SB:4 * D_SB]
    v_new = p3[:, :, 4 * D_SB:5 * D_SB]
    pad_rows = lambda a: jnp.pad(a, ((0, 0), (0, PAGE_SIZE - t_len), (0, 0)))
    bias_rows = jnp.broadcast_to(jnp.repeat(w['sb_bias'], t_len)[:, None], (SB_HEADS * t_len, PAGE_SIZE))
    as_page = lambda a: pad_rows(a).transpose(0, 2, 1)
    ob = _sb_sample(q, bias_rows, as_page(k_new), as_page(v_new), cache_k, cache_v, page_table, layer=layer)
    ob = ob.reshape(m, D_SB)
    yc, s_new = _gdn(proj, tail, GDN_COL0, w['conv_gdn_w'], gdn_buf, w['gdn_a_log'], w['gdn_dt_bias'], gdn_s,
                     w['g_gdn_norm'], nb=nb, t_len=t_len, rows=t_len, chunk=128, valid_rows=t_len)
    x1 = _out_proj(ya, ob, yc, x, mod_bm, w['g_grp_sb'], w['g_post_mix'], w['w_out'], layer=layer, tm=m,
                   tiles_per_group=1)
    to_tm = lambda a: a.reshape(nb, -1, a.shape[-1]).transpose(1, 0, 2).reshape(-1, a.shape[-1])
    to_bm = lambda a: a.reshape(-1, nb, a.shape[-1]).transpose(1, 0, 2)
    x2_tm, ffn_conv_tm = _ffn(to_tm(x1), w['g_pre_ffn'], mod_tm, w['g_post_ffn'], w['w_ffn_up'], w['w_ffn_down'],
                              w['conv_ffn_w'], w['conv_ffn_b'], to_tm(ffn_buf)[None],
                              layer=layer, tm=m, rpt=nb, tiles_per_group=1, tf=512)
    x2 = to_bm(x2_tm).reshape(m, D_MODEL)
    ffn_conv = to_bm(ffn_conv_tm[0])
    lru_conv = p3[:, t_len - (CONV_W - 1):, 0:D_LRU]
    gdn_conv = p3[:, t_len - (CONV_W - 1):, GDN_COL0 * 128:GDN_COL0 * 128 + 3 * D_GDN]
    hd = (nb, t_len, SB_HEADS, SB_HEAD_DIM)
    return x2, (k_new.reshape(hd), v_new.reshape(hd), lru_conv, h_last.reshape(nb, D_LRU), gdn_conv, s_new, ffn_conv)


def kernel(x_prompt, x_sample, c_prompt, c_sample, cache_sb_k, cache_sb_v, page_table, state_lru_conv, state_lru_h, state_gdn_conv, state_gdn, state_ffn_conv, w_ada, b_ada, g_pre_mix, g_post_mix, g_pre_ffn, g_post_ffn, w_in, conv_lru_w, conv_lru_b, w_lru_r, b_lru_r, w_lru_i, b_lru_i, lru_lambda, g_grp_lru, g_grp_sb, sb_bias, conv_gdn_w, gdn_a_log, gdn_dt_bias, g_gdn_norm, w_out, w_ffn_up, conv_ffn_w, conv_ffn_b, w_ffn_down):
    bp, t_p, d = x_prompt.shape
    bs, t_s, _ = x_sample.shape
    depth = w_ada.shape[0]
    n_pool = cache_sb_k.shape[1]
    assert bp + bs <= MOD_ROWS and d == D_MODEL

    c_all = jnp.concatenate([c_prompt, c_sample, jnp.zeros((MOD_ROWS - bp - bs, d), F32)], axis=0)
    mod_all = _ada(c_all, w_ada, b_ada).reshape(depth, MOD_ROWS, N_MOD, d)
    as_pages = lambda c: c.transpose(0, 1, 3, 4, 2).reshape(depth, n_pool, D_SB, PAGE_SIZE)
    cache_k = as_pages(cache_sb_k)
    cache_v = as_pages(cache_sb_v)

    params = dict(g_pre_mix=g_pre_mix, g_post_mix=g_post_mix, g_pre_ffn=g_pre_ffn, g_post_ffn=g_post_ffn,
                  conv_lru_w=conv_lru_w, conv_lru_b=conv_lru_b, w_lru_r=w_lru_r, b_lru_r=b_lru_r, w_lru_i=w_lru_i,
                  b_lru_i=b_lru_i, lru_lambda=lru_lambda, g_grp_lru=g_grp_lru, g_grp_sb=g_grp_sb, sb_bias=sb_bias,
                  conv_gdn_w=conv_gdn_w, gdn_a_log=gdn_a_log, gdn_dt_bias=gdn_dt_bias, g_gdn_norm=g_gdn_norm,
                  conv_ffn_w=conv_ffn_w, conv_ffn_b=conv_ffn_b)
    big = dict(w_in=w_in.astype(BF16),
               w_tail=jnp.pad(w_in[:, :, D_MAIN:], ((0, 0), (0, 0), (0, 128 - D_TAIL))).astype(BF16),
               w_out=w_out.astype(BF16), w_ffn_up=w_ffn_up.astype(BF16), w_ffn_down=w_ffn_down.astype(BF16))

    xp = x_prompt.reshape(bp * t_p, d)
    xs = x_sample.reshape(bs * t_s, d)
    outs_p, outs_s = [], []
    for l in range(depth):
        w = _prep_layer_weights({k: v[l] for k, v in params.items()})
        w.update(big, layer=l)
        mod_l = mod_all[l]
        mod_p = mod_l[:bp].transpose(1, 0, 2)[:, :, None, :]
        mod_s = mod_l[bp:bp + bs].transpose(1, 0, 2)
        mod_s_bm = jnp.repeat(mod_s, t_s, axis=1)[:, None]
        mod_s_tm = jnp.tile(mod_s, (1, t_s, 1))[:, None]
        xp, op = _layer_prompt(xp, mod_p, w, nb=bp, t_len=t_p)
        xs, os_ = _layer_sample(xs, mod_s_bm, mod_s_tm, w, l, cache_k, cache_v, page_table,
                                state_lru_conv[l], state_lru_h[l], state_gdn_conv[l], state_gdn[l],
                                state_ffn_conv[l], nb=bs, t_len=t_s)
        outs_p.append(op)
        outs_s.append(os_)

    res = [xp.reshape(bp, t_p, d), xs.reshape(bs, t_s, d)]
    order = (0, 1, 2, 3, 4, 5, 6)
    for idx in order:
        res.append(jnp.stack([o[idx] for o in outs_p]))
        res.append(jnp.stack([o[idx] for o in outs_s]))
    y_p, y_s, kp, ks, vp, vs = res[0], res[1], res[2], res[3], res[4], res[5]
    return (y_p, y_s, kp, vp, ks, vs) + tuple(res[6:])
```

```python
import functools
import math

import jax
import jax.numpy as jnp
from jax import lax
from jax.experimental import pallas as pl
from jax.experimental.pallas import tpu as pltpu

F32 = jnp.float32
BF16 = jnp.bfloat16

D_MODEL = 2048
D_LRU = 512
LRU_BLOCKS = 8
LRU_C = 8.0
CONV_W = 4
D_SB = 512
SB_HEADS = 8
SB_HEAD_DIM = 64
D_GDN = 1024
GDN_HEADS = 8
GDN_HEAD_DIM = 128
D_FF = 5632
FFN_CONV_W = 3
N_MOD = 6
EPS = 1e-6
PAGE_SIZE = 128
D_MAIN = 2 * D_LRU + 3 * D_SB + 4 * D_GDN
D_TAIL = 2 * GDN_HEADS
GDN_COL0 = (2 * D_LRU + 3 * D_SB) // 128

VMEM_LIMIT = 56 * 1024 * 1024
MOD_ROWS = 16
SB_PAGES_PER_STEP = 16
SB_KEY_BLOCK = 256
SB_LOCKSTEP = 8
GDN_HEADS_PER_STEP = 4
GDN_CHUNK = 128
ROW_TILE = 512
IN_PROJ_ROW_TILE = 1024
COL_TILE = 512
SEQ_TILE = 256


def _cparams(sem):
    return pltpu.CompilerParams(dimension_semantics=sem, vmem_limit_bytes=VMEM_LIMIT)


def _dot(a, b):
    return jnp.dot(a, b, preferred_element_type=F32)


def _dot_nt(a, b):
    return lax.dot_general(a, b, (((1,), (1,)), ((), ())), preferred_element_type=F32)


def _split(a):
    hi = a.astype(BF16)
    lo = (a - hi.astype(F32)).astype(BF16)
    return hi, lo


def _mm3s(a, b):
    return _dot(a[0], b[0]) + (_dot(a[0], b[1]) + _dot(a[1], b[0]))


def _mm_exact_lhs(a_bf16, b):
    b_hi, b_lo = _split(b)
    return _dot(a_bf16, b_hi) + _dot(a_bf16, b_lo)


LOG2E = 1.4426950408889634


def _softplus(z):
    return jnp.maximum(z, 0.0) + jnp.log(1.0 + jnp.exp2(jnp.abs(z) * (-LOG2E)))


def _sigmoid(z):
    return jax.nn.sigmoid(z)


def _rms(x, g):
    return x * lax.rsqrt(jnp.mean(x * x, axis=-1, keepdims=True) + EPS) * g


def _ada_kernel(c_ref, w_ref, b_ref, o_ref):
    c = c_ref[...]
    s = c * _sigmoid(c)
    o_ref[0] = _dot(s.astype(BF16), w_ref[0].astype(BF16)) + b_ref[0]


def _ada(c_all, w_ada, b_ada):
    depth, d, n = w_ada.shape
    tn = 1024
    return pl.pallas_call(
        _ada_kernel,
        grid=(depth, n // tn),
        in_specs=[pl.BlockSpec((MOD_ROWS, d), lambda l, j: (0, 0)),
                  pl.BlockSpec((1, d, tn), lambda l, j: (l, 0, j)),
                  pl.BlockSpec((1, 1, tn), lambda l, j: (l, 0, j))],
        out_specs=pl.BlockSpec((1, MOD_ROWS, tn), lambda l, j: (l, 0, j)),
        out_shape=jax.ShapeDtypeStruct((depth, MOD_ROWS, n), F32),
        compiler_params=_cparams(("arbitrary", "arbitrary")),
        name="ada",
    )(c_all, w_ada, b_ada.reshape(depth, 1, n))


def _in_proj_kernel(x_ref, g_ref, sh_ref, sc_ref, w_ref, wt_ref, o_ref, ot_ref, h_scr):
    j = pl.program_id(1)

    @pl.when(j == 0)
    def _():
        h = _rms(x_ref[...], g_ref[...]) * (1.0 + sc_ref[0, 0]) + sh_ref[0, 0]
        hb = h.astype(BF16)
        h_scr[...] = hb
        ot_ref[...] = _dot(hb, wt_ref[...])

    o_ref[...] = _dot(h_scr[...], w_ref[...])


def _in_proj(x, g, mod, w_all, w_tail, *, layer, tm, tiles_per_group):
    m, d = x.shape
    n = D_MAIN
    tn = COL_TILE
    mrows = mod.shape[2]
    mod_spec = lambda k: pl.BlockSpec((1, 1, mrows, d), lambda i, j: (k, i // tiles_per_group, 0, 0))
    return pl.pallas_call(
        _in_proj_kernel,
        grid=(m // tm, n // tn),
        in_specs=[pl.BlockSpec((tm, d), lambda i, j: (i, 0)),
                  pl.BlockSpec((1, d), lambda i, j: (0, 0)),
                  mod_spec(0), mod_spec(1),
                  pl.BlockSpec((None, d, tn), lambda i, j: (layer, 0, j)),
                  pl.BlockSpec((None, d, 128), lambda i, j: (layer, 0, 0))],
        out_specs=[pl.BlockSpec((tm, tn), lambda i, j: (i, j)),
                   pl.BlockSpec((tm, 128), lambda i, j: (i, 0))],
        out_shape=[jax.ShapeDtypeStruct((m, n), F32),
                   jax.ShapeDtypeStruct((m, 128), F32)],
        scratch_shapes=[pltpu.VMEM((tm, d), BF16)],
        compiler_params=_cparams(("arbitrary", "arbitrary")),
        name="in_proj",
    )(x, g, mod, mod, w_all, w_tail)


def _lru_kernel(ax_ref, ag_ref, buf_ref, h0_ref, cw_ref, cb_ref, wr_ref, br_ref, wi_ref, bi_ref, lam_ref, gg_ref,
                y_ref, hl_ref, xp_scr, h_scr, *, rows):
    t = pl.program_id(1)

    @pl.when(t == 0)
    def _():
        xp_scr[5:8, :] = buf_ref[0]
        h_scr[...] = h0_ref[0]

    x = ax_ref[...]
    xp_scr[8:8 + rows, :] = x
    cw = cw_ref[...]
    xc = (xp_scr[5:5 + rows, :] * cw[0:1] + xp_scr[6:6 + rows, :] * cw[1:2]
          + xp_scr[7:7 + rows, :] * cw[2:3] + x * cw[3:4]) + cb_ref[...]
    xp_scr[5:8, :] = xp_scr[5 + rows:8 + rows, :]

    xb = xc.astype(BF16)
    r = _sigmoid(_dot(xb, wr_ref[...]) + br_ref[...])
    ig = _sigmoid(_dot(xb, wi_ref[...]) + bi_ref[...])
    log_a = (LRU_C * r) * (-_softplus(-lam_ref[...]))
    a = jnp.exp(log_a)
    u = jnp.sqrt(1.0 - a * a) * (ig * xc)

    ridx = lax.broadcasted_iota(jnp.int32, a.shape, 0)
    s = 1
    while s < rows:
        a_sh = pltpu.roll(a, s, 0)
        u_sh = pltpu.roll(u, s, 0)
        m = ridx >= s
        u = jnp.where(m, a * u_sh + u, u)
        a = jnp.where(m, a * a_sh, a)
        s *= 2
    h = a * h_scr[...] + u
    h_last = h[rows - 1:rows, :]
    h_scr[...] = h_last
    hl_ref[0] = h_last

    ag = ag_ref[...]
    gelu = 0.5 * ag * (1.0 + jnp.tanh(math.sqrt(2.0 / math.pi) * (ag + 0.044715 * (ag * ag * ag))))
    y_ref[...] = _rms(h * gelu, gg_ref[...])


def _lru(proj, buf, h0, cw, cb, wr, br, wi, bi, lam, gg, *, nb, t_len, rows):
    m = proj.shape[0]
    nt = t_len // rows
    row_map = lambda c: (lambda b, t: (b * nt + t, c))
    vec = lambda: pl.BlockSpec((1, D_LRU), lambda b, t: (0, 0))
    return pl.pallas_call(
        functools.partial(_lru_kernel, rows=rows),
        grid=(nb, nt),
        in_specs=[pl.BlockSpec((rows, D_LRU), row_map(0)),
                  pl.BlockSpec((rows, D_LRU), row_map(1)),
                  pl.BlockSpec((1, CONV_W - 1, D_LRU), lambda b, t: (b, 0, 0)),
                  pl.BlockSpec((1, 1, D_LRU), lambda b, t: (b, 0, 0)),
                  pl.BlockSpec((CONV_W, D_LRU), lambda b, t: (0, 0)),
                  vec(),
                  pl.BlockSpec((D_LRU, D_LRU), lambda b, t: (0, 0)), vec(),
                  pl.BlockSpec((D_LRU, D_LRU), lambda b, t: (0, 0)), vec(),
                  vec(), vec()],
        out_specs=[pl.BlockSpec((rows, D_LRU), lambda b, t: (b * nt + t, 0)),
                   pl.BlockSpec((1, 1, D_LRU), lambda b, t: (b, 0, 0))],
        out_shape=[jax.ShapeDtypeStruct((m, D_LRU), F32),
                   jax.ShapeDtypeStruct((nb, 1, D_LRU), F32)],
        scratch_shapes=[pltpu.VMEM((rows + 8, D_LRU), F32), pltpu.VMEM((1, D_LRU), F32)],
        compiler_params=_cparams(("arbitrary", "arbitrary")),
        name="lru",
    )(proj, proj, buf, h0, cw, cb, wr, br, wi, bi, lam, gg)


def _neg_tri2(n):
    row = lax.broadcasted_iota(jnp.int32, (2 * n, n), 0) & (n - 1)
    col = lax.broadcasted_iota(jnp.int32, (2 * n, n), 1)
    return jnp.where(row >= col, -1.0, 0.0).astype(BF16)


def _sb_stages(zs, mask, ntri2, carries):
    ss = [_softplus(z) for z in zs]
    if mask is not None:
        ss = [jnp.where(mask, s, 0.0) for s in ss]
    parts = [_split(s) for s in ss]
    after = [_dot(jnp.concatenate([hi, lo], axis=1), ntri2) for hi, lo in parts]
    ws = [jnp.exp(z + (a + c)) for z, a, c in zip(zs, after, carries)]
    if mask is not None:
        ws = [jnp.where(mask, w, 0.0) for w in ws]
    return ws, [jnp.sum(s, axis=1, keepdims=True) for s in ss]


def _sb_prompt_kernel(bias_ref, q_ref, k_ref, v_ref, o_ref, car_scr, *, tq):
    qi = pl.program_id(1)
    scale = SB_HEAD_DIM ** -0.5
    ntri2 = _neg_tri2(tq)
    row = lax.broadcasted_iota(jnp.int32, (tq, tq), 0)
    col = lax.broadcasted_iota(jnp.int32, (tq, tq), 1)
    causal = col < row
    lo_half = lax.broadcasted_iota(jnp.int32, (tq, 128), 1) < SB_HEAD_DIM
    npair = SB_HEADS // 2

    qm = []
    for pr in range(npair):
        qp = q_ref[:, pr * 128:(pr + 1) * 128] * scale
        qm.append((jnp.where(lo_half, qp, 0.0).astype(BF16), jnp.where(lo_half, 0.0, qp).astype(BF16)))
    o_ref[...] = jnp.zeros_like(o_ref)
    car_scr[...] = jnp.zeros_like(car_scr)

    def blocks(kb, mask):
        start = pl.multiple_of(kb * tq, tq)
        for grp in range(SB_HEADS // SB_LOCKSTEP):
            hs = list(range(grp * SB_LOCKSTEP, (grp + 1) * SB_LOCKSTEP))
            kp, vm = {}, {}
            for pr in sorted({h // 2 for h in hs}):
                cols = slice(pr * 128, (pr + 1) * 128)
                kp[pr] = k_ref[pl.ds(start, tq), cols].astype(BF16)
                vp = v_ref[pl.ds(start, tq), cols]
                vm[pr] = (jnp.where(lo_half, vp, 0.0).astype(BF16), jnp.where(lo_half, 0.0, vp).astype(BF16))
            z = [_dot_nt(qm[h // 2][h % 2], kp[h // 2]) + bias_ref[h] for h in hs]
            car = [car_scr[h] for h in hs]
            w, tot = _sb_stages(z, mask, ntri2, car)
            for h, c, t_ in zip(hs, car, tot):
                car_scr[h] = c - t_
            o = [_dot(x.astype(BF16), vm[h // 2][h % 2]) for x, h in zip(w, hs)]
            c0 = (hs[0] // 2) * 128
            o_ref[:, c0:c0 + 64 * len(hs)] += jnp.concatenate(
                [o[i] + o[i + 1] for i in range(0, len(hs), 2)], axis=1)

    blocks(qi, causal)

    def body(it, c):
        blocks(qi - 1 - it, None)
        return c

    lax.fori_loop(0, qi, body, 0)


def _sb_prompt(proj, bias, *, nb, t_len, tq):
    m = proj.shape[0]
    nq = t_len // tq
    qcol, kcol, vcol = (2 * D_LRU) // D_SB, (2 * D_LRU) // D_SB + 1, (2 * D_LRU) // D_SB + 2
    return pl.pallas_call(
        functools.partial(_sb_prompt_kernel, tq=tq),
        grid=(nb, nq),
        in_specs=[pl.BlockSpec(memory_space=pltpu.SMEM),
                  pl.BlockSpec((tq, D_SB), lambda b, i: (b * nq + i, qcol)),
                  pl.BlockSpec((t_len, D_SB), lambda b, i: (b, kcol)),
                  pl.BlockSpec((t_len, D_SB), lambda b, i: (b, vcol))],
        out_specs=pl.BlockSpec((tq, D_SB), lambda b, i: (b * nq + i, 0)),
        out_shape=jax.ShapeDtypeStruct((m, D_SB), F32),
        scratch_shapes=[pltpu.VMEM((SB_HEADS, tq, 1), F32)],
        compiler_params=_cparams(("arbitrary", "arbitrary")),
        name="sb_prompt",
    )(bias, proj, proj, proj)


def _sb_sample_kernel(pt_ref, q_ref, bias_ref, kn_ref, vn_ref, *rest, t_new, group):
    del pt_ref
    kc_refs, vc_refs = rest[:group], rest[group:2 * group]
    o_ref, acc_scr, car_scr = rest[2 * group:]
    p = pl.program_id(1)
    rows = SB_HEADS * t_new
    scale = SB_HEAD_DIM ** -0.5
    q = q_ref[0] * scale
    q_rep = jnp.concatenate([q] * SB_HEADS, axis=0)
    rh = lax.broadcasted_iota(jnp.int32, (rows, D_SB), 0) >> int(math.log2(t_new))
    ch = lax.broadcasted_iota(jnp.int32, (rows, D_SB), 1) >> int(math.log2(SB_HEAD_DIM))
    own = rh == ch
    qbd = jnp.where(own, q_rep, 0.0).astype(BF16)
    bias = bias_ref[...]

    def keys(kt, vt, width, mask):
        n = kt.shape[1]
        nblk = n // width
        z = _dot(qbd, kt) + jnp.concatenate([bias] * (n // PAGE_SIZE), axis=1)
        s = _softplus(z)
        if mask is not None:
            s = jnp.where(mask, s, 0.0)
        sk = [s[:, k * width:(k + 1) * width] for k in range(nblk)]
        tot = [jnp.sum(x, axis=1, keepdims=True) for x in sk]
        hi, lo = _split(jnp.concatenate(sk, axis=0) if nblk > 1 else sk[0])
        aft = _dot(jnp.concatenate([hi, lo], axis=1), _neg_tri2(width))
        car = car_scr[...]
        ws = [None] * nblk
        for k in reversed(range(nblk)):
            ws[k] = jnp.exp(z[:, k * width:(k + 1) * width] + (aft[k * rows:(k + 1) * rows] + car))
            car = car - tot[k]
        w = jnp.concatenate(ws, axis=1) if nblk > 1 else ws[0]
        if mask is not None:
            w = jnp.where(mask, w, 0.0)
        car_scr[...] = car
        acc_scr[...] += _dot_nt(w.astype(BF16), vt)

    @pl.when(p == 0)
    def _():
        acc_scr[...] = jnp.zeros_like(acc_scr)
        car_scr[...] = jnp.zeros_like(car_scr)
        qpos = lax.broadcasted_iota(jnp.int32, (rows, PAGE_SIZE), 0) & (t_new - 1)
        kpos = lax.broadcasted_iota(jnp.int32, (rows, PAGE_SIZE), 1)
        keys(kn_ref[0].astype(BF16), vn_ref[0].astype(BF16), PAGE_SIZE, kpos < qpos)

    keys(jnp.concatenate([r[...].astype(BF16) for r in kc_refs], axis=1),
         jnp.concatenate([r[...].astype(BF16) for r in vc_refs], axis=1), SB_KEY_BLOCK, None)

    @pl.when(p == pl.num_programs(1) - 1)
    def _():
        acc = jnp.where(own, acc_scr[...], 0.0)
        out = acc[0:t_new]
        for h in range(1, SB_HEADS):
            out = out + acc[h * t_new:(h + 1) * t_new]
        o_ref[0] = out


def _sb_sample(q, bias_rows, k_new, v_new, cache_k, cache_v, page_table, *, layer):
    nb, t_new, _ = q.shape
    n_pages = page_table.shape[1]
    rows = SB_HEADS * t_new
    group = SB_PAGES_PER_STEP
    assert n_pages % group == 0

    def cache_spec(g):
        return pl.BlockSpec((None, None, D_SB, PAGE_SIZE),
                            lambda b, p, pt: (layer, pt[b, n_pages - group * (p + 1) + g], 0, 0))

    grid_spec = pltpu.PrefetchScalarGridSpec(
        num_scalar_prefetch=1,
        grid=(nb, n_pages // group),
        in_specs=[pl.BlockSpec((1, t_new, D_SB), lambda b, p, pt: (b, 0, 0)),
                  pl.BlockSpec((rows, PAGE_SIZE), lambda b, p, pt: (0, 0)),
                  pl.BlockSpec((1, D_SB, PAGE_SIZE), lambda b, p, pt: (b, 0, 0)),
                  pl.BlockSpec((1, D_SB, PAGE_SIZE), lambda b, p, pt: (b, 0, 0))]
                 + [cache_spec(g) for g in range(group)] * 2,
        out_specs=pl.BlockSpec((1, t_new, D_SB), lambda b, p, pt: (b, 0, 0)),
        scratch_shapes=[pltpu.VMEM((rows, D_SB), F32), pltpu.VMEM((rows, 1), F32)],
    )
    return pl.pallas_call(
        functools.partial(_sb_sample_kernel, t_new=t_new, group=group),
        grid_spec=grid_spec,
        out_shape=jax.ShapeDtypeStruct((nb, t_new, D_SB), F32),
        compiler_params=_cparams(("arbitrary", "arbitrary")),
        name="sb_sample",
    )(page_table, q, bias_rows, k_new, v_new, *([cache_k] * group), *([cache_v] * group))


def _unit_lower_inverses(lmats, c, support):
    ii = lax.broadcasted_iota(jnp.int32, (c, c), 0)
    jj = lax.broadcasted_iota(jnp.int32, (c, c), 1)
    eye = jnp.where(ii == jj, 1.0, 0.0)
    base = min(16, c)
    shift = int(math.log2(base))
    same_base = (ii >> shift) == (jj >> shift)
    pws = [jnp.where(same_base, l, 0.0) for l in lmats]
    xs = [eye - p for p in pws]
    ps = [_split(p) for p in pws]
    n = 2
    while n < min(base, support):
        ps = [_split(_mm3s(p, p)) for p in ps]
        xs = [x + _mm3s(_split(x), p) for x, p in zip(xs, ps)]
        n *= 2
    s = base
    while s < min(c, support):
        sh = int(math.log2(s))
        below = ((ii >> (sh + 1)) == (jj >> (sh + 1))) & ((ii >> sh) != (jj >> sh))
        xsp = [_split(x) for x in xs]
        ys = [_mm3s(x, _split(jnp.where(below, l, 0.0))) for x, l in zip(xsp, lmats)]
        xs = [x - _mm3s(_split(y), xp) for x, y, xp in zip(xs, ys, xsp)]
        s *= 2
    return xs


def _gdn_kernel(q_ref, k_ref, v_ref, z_ref, tail_ref, cwq_ref, cwk_ref, cwv_ref, bq_ref, bk_ref, bv_ref,
                alog_ref, dtb_ref, s0_ref, gn_ref, y_ref, s_ref, xq_scr, xk_scr, xv_scr,
                *, rows, chunk, valid_rows, hg):
    t = pl.program_id(2)
    dk = GDN_HEAD_DIM

    @pl.when(t == 0)
    def _():
        xq_scr[5:8, :] = bq_ref[0]
        xk_scr[5:8, :] = bk_ref[0]
        xv_scr[5:8, :] = bv_ref[0]
        s_ref[0] = s0_ref[0]

    def conv_silu(x_ref, scr, cw_ref):
        x = x_ref[...]
        scr[8:8 + rows, :] = x
        cw = cw_ref[...]
        y = (scr[5:5 + rows, :] * cw[0:1] + scr[6:6 + rows, :] * cw[1:2]
             + scr[7:7 + rows, :] * cw[2:3] + x * cw[3:4])
        scr[5:8, :] = scr[5 + rows:8 + rows, :]
        return y * _sigmoid(y)

    q_all = conv_silu(q_ref, xq_scr, cwq_ref)
    k_all = conv_silu(k_ref, xk_scr, cwk_ref)
    v_all = conv_silu(v_ref, xv_scr, cwv_ref)

    tail = tail_ref[...]
    crows = -(-rows // chunk) * chunk
    if crows > rows:
        assert valid_rows is not None and valid_rows <= rows
        pad0 = lambda a: jnp.concatenate([a, jnp.zeros((crows - rows, a.shape[1]), F32)], axis=0)
        q_all, k_all, v_all, tail = pad0(q_all), pad0(k_all), pad0(v_all), pad0(tail)
    lane = lax.broadcasted_iota(jnp.int32, tail.shape, 1)
    lane1 = lax.broadcasted_iota(jnp.int32, (1, 128), 1)
    ii = lax.broadcasted_iota(jnp.int32, (chunk, chunk), 0)
    jj = lax.broadcasted_iota(jnp.int32, (chunk, chunk), 1)
    incl = ii >= jj
    strict = ii > jj
    ltri = jnp.where(incl, 1.0, 0.0).astype(BF16)
    support = chunk
    if valid_rows is not None:
        ok = lax.broadcasted_iota(jnp.int32, (crows, 1), 0) + t * rows < valid_rows
        support = min(chunk, 1 << max(valid_rows - 1, 0).bit_length())

    nchunk = crows // chunk
    heads = list(range(hg))
    chains = [(hh, c) for hh in heads for c in range(nchunk)]
    qn, kn, beta, g = [], [], [], []
    for hh in heads:
        h = pl.program_id(1) * hg + hh
        cs = slice(hh * dk, (hh + 1) * dk)
        q, k = q_all[:, cs], k_all[:, cs]
        qn.append(q * lax.rsqrt(jnp.sum(q * q, axis=-1, keepdims=True) + EPS) * (dk ** -0.5))
        kn.append(k * lax.rsqrt(jnp.sum(k * k, axis=-1, keepdims=True) + EPS))
        b_col = jnp.sum(jnp.where(lane == h, tail, 0.0), axis=1, keepdims=True)
        a_col = jnp.sum(jnp.where(lane == GDN_HEADS + h, tail, 0.0), axis=1, keepdims=True)
        a_log = jnp.sum(jnp.where(lane1 == h, alog_ref[...], 0.0), axis=1, keepdims=True)
        dt_b = jnp.sum(jnp.where(lane1 == h, dtb_ref[...], 0.0), axis=1, keepdims=True)
        b_h = _sigmoid(b_col)
        g_h = -jnp.exp(a_log) * _softplus(a_col + dt_b)
        if valid_rows is not None:
            b_h = jnp.where(ok, b_h, 0.0)
            g_h = jnp.where(ok, g_h, 0.0)
        beta.append(b_h)
        g.append(g_h)

    def per_chain(per_head, lanes=False):
        out = []
        for hh, c in chains:
            a = per_head[hh]
            if lanes:
                a = a[:, hh * dk:(hh + 1) * dk]
            out.append(a[c * chunk:(c + 1) * chunk])
        return out

    g_c, beta_c, q_c, k_c = per_chain(g), per_chain(beta), per_chain(qn), per_chain(kn)
    v_c = per_chain([v_all] * hg, lanes=True)
    gcum = [_mm_exact_lhs(ltri, jnp.broadcast_to(x, (chunk, dk))) for x in g_c]
    gdiff = [_mm_exact_lhs(ltri, jnp.where(strict, jnp.broadcast_to(x, (chunk, chunk)), 0.0)) for x in g_c]
    decay = [jnp.where(incl, jnp.exp(x), 0.0) for x in gdiff]
    kb = [a * b for a, b in zip(k_c, beta_c)]
    k_bf = [a.astype(BF16) for a in k_c]
    lmat = [jnp.where(strict, _dot_nt(a.astype(BF16), kk) * d, 0.0) for a, kk, d in zip(kb, k_bf, decay)]
    qk = [(_dot_nt(a.astype(BF16), kk) * d).astype(BF16) for a, kk, d in zip(q_c, k_bf, decay)]
    tinv = [x.astype(BF16) for x in _unit_lower_inverses(lmat, chunk, support)]
    egc = [jnp.exp(x) for x in gcum]
    u = [_dot(ti, (a * b).astype(BF16)) for ti, a, b in zip(tinv, v_c, beta_c)]
    w = [_dot(ti, (a * e).astype(BF16)).astype(BF16) for ti, a, e in zip(tinv, kb, egc)]
    g_last = [x[chunk - 1:chunk, :] for x in gcum]
    qg = [(a * e).astype(BF16) for a, e in zip(q_c, egc)]
    kd_t = [(a * jnp.exp(gl - gc)).T.astype(BF16) for a, gl, gc in zip(k_c, g_last, gcum)]
    eg = [jnp.exp(gl) for gl in g_last]

    gn = gn_ref[...]
    s = [s_ref[0, hh] for hh in heads]
    outs = [[] for _ in heads]
    for c in range(nchunk):
        idx = [hh * nchunk + c for hh in heads]
        sb = [x.astype(BF16) for x in s]
        v_new = [u[i] - _dot(w[i], sb[hh]) for hh, i in zip(heads, idx)]
        o_state = [_dot(qg[i], sb[hh]) for hh, i in zip(heads, idx)]
        vb = [x.astype(BF16) for x in v_new]
        o = [o_state[hh] + _dot(qk[i], vb[hh]) for hh, i in zip(heads, idx)]
        s = [s[hh] * eg[i] + _dot(kd_t[i], vb[hh]) for hh, i in zip(heads, idx)]
        for hh in heads:
            outs[hh].append(_rms(o[hh], gn))
    ys = [jnp.concatenate(o, axis=0) if nchunk > 1 else o[0] for o in outs]
    z_all = z_ref[...]
    y_ref[...] = jnp.concatenate(ys, axis=1)[:rows] * (z_all * _sigmoid(z_all))
    s_ref[0] = jnp.stack(s, axis=0)


def _gdn(arr, tail, col0, cw, buf, a_log, dt_bias, s0, gn, *, nb, t_len, rows, chunk, valid_rows):
    m = arr.shape[0]
    nt = t_len // rows
    hd = GDN_HEAD_DIM
    nh = GDN_HEADS
    hg = GDN_HEADS_PER_STEP
    wd = hg * hd
    assert nh % hg == 0 and col0 % hg == 0
    col = lambda base: pl.BlockSpec((rows, wd), lambda b, h, t: (b * nt + t, (col0 + base) // hg + h))
    cwspec = lambda base: pl.BlockSpec((CONV_W, wd), lambda b, h, t: (0, base // hg + h))
    bufspec = lambda base: pl.BlockSpec((1, CONV_W - 1, wd), lambda b, h, t: (b, 0, base // hg + h))
    vec = lambda: pl.BlockSpec((1, 128), lambda b, h, t: (0, 0))
    sspec = lambda: pl.BlockSpec((1, hg, hd, hd), lambda b, h, t: (b, h, 0, 0))
    return pl.pallas_call(
        functools.partial(_gdn_kernel, rows=rows, chunk=chunk, valid_rows=valid_rows, hg=hg),
        grid=(nb, nh // hg, nt),
        in_specs=[col(0), col(nh), col(2 * nh), col(3 * nh),
                  pl.BlockSpec((rows, 128), lambda b, h, t: (b * nt + t, 0)),
                  cwspec(0), cwspec(nh), cwspec(2 * nh),
                  bufspec(0), bufspec(nh), bufspec(2 * nh),
                  vec(), vec(), sspec(), vec()],
        out_specs=[pl.BlockSpec((rows, wd), lambda b, h, t: (b * nt + t, h)), sspec()],
        out_shape=[jax.ShapeDtypeStruct((m, D_GDN), F32),
                   jax.ShapeDtypeStruct((nb, nh, hd, hd), F32)],
        scratch_shapes=[pltpu.VMEM((rows + 8, wd), F32)] * 3,
        compiler_params=_cparams(("arbitrary", "arbitrary", "arbitrary")),
        name="gdn",
    )(arr, arr, arr, arr, tail, cw, cw, cw, buf, buf, buf, a_log, dt_bias, s0, gn)


def _out_proj_kernel(ya_ref, ob_ref, yc_ref, x_ref, gate_ref, gsb_ref, gpost_ref, w_ref, o_ref):
    yb = _rms(ob_ref[...], gsb_ref[...])
    mix = (_dot(ya_ref[...].astype(BF16), w_ref[0:D_LRU, :])
           + _dot(yb.astype(BF16), w_ref[D_LRU:D_LRU + D_SB, :])
           + _dot(yc_ref[...].astype(BF16), w_ref[D_LRU + D_SB:, :]))
    o_ref[...] = x_ref[...] + gate_ref[0, 0] * _rms(mix, gpost_ref[...])


def _out_proj(ya, ob, yc, x, mod, gsb, gpost, w_out, *, layer, tm, tiles_per_group):
    m, d = x.shape
    mrows = mod.shape[2]
    return pl.pallas_call(
        _out_proj_kernel,
        grid=(m // tm,),
        in_specs=[pl.BlockSpec((tm, D_LRU), lambda i: (i, 0)),
                  pl.BlockSpec((tm, D_SB), lambda i: (i, 0)),
                  pl.BlockSpec((tm, D_GDN), lambda i: (i, 0)),
                  pl.BlockSpec((tm, d), lambda i: (i, 0)),
                  pl.BlockSpec((1, 1, mrows, d), lambda i: (2, i // tiles_per_group, 0, 0)),
                  pl.BlockSpec((1, D_SB), lambda i: (0, 0)),
                  pl.BlockSpec((1, d), lambda i: (0, 0)),
                  pl.BlockSpec((None, d, d), lambda i: (layer, 0, 0))],
        out_specs=pl.BlockSpec((tm, d), lambda i: (i, 0)),
        out_shape=jax.ShapeDtypeStruct((m, d), F32),
        compiler_params=_cparams(("arbitrary",)),
        name="out_proj",
    )(ya, ob, yc, x, mod, gsb, gpost, w_out)


def _ffn_kernel(x_ref, g_ref, sh_ref, sc_ref, gate_ref, gpost_ref, wg_ref, wv_ref, wd_ref, cw_ref, cb_ref, cin_ref,
                o_ref, cout_ref, h_scr, acc_scr, gbuf_scr, carry_scr, *, tm, rpt, tiles_per_group):
    i = pl.program_id(0)
    j = pl.program_id(1)
    pad = gbuf_scr.shape[0] - tm
    keep = (FFN_CONV_W - 1) * rpt

    @pl.when(j == 0)
    def _():
        h = _rms(x_ref[...], g_ref[...]) * (1.0 + sc_ref[0, 0]) + sh_ref[0, 0]
        h_scr[...] = h.astype(BF16)
        acc_scr[...] = jnp.zeros_like(acc_scr)

    first = (i % tiles_per_group) == 0

    @pl.when(first)
    def _():
        gbuf_scr[pad - keep:pad, :] = cin_ref[0]

    @pl.when(jnp.logical_not(first))
    def _():
        gbuf_scr[pad - keep:pad, :] = carry_scr[j]

    hb = h_scr[...]
    gate = _dot(hb, wg_ref[...])
    val = _dot(hb, wv_ref[...])
    gbuf_scr[pad:pad + tm, :] = gate
    cw = cw_ref[...]
    gc = (gbuf_scr[pad - 2 * rpt:pad - 2 * rpt + tm, :] * cw[0:1]
          + gbuf_scr[pad - rpt:pad - rpt + tm, :] * cw[1:2] + gate * cw[2:3]) + cb_ref[...]
    last = gbuf_scr[pad + tm - keep:pad + tm, :]
    carry_scr[j] = last
    cout_ref[0] = last
    f = (gc * _sigmoid(gc)) * val
    acc_scr[...] += _dot(f.astype(BF16), wd_ref[...])

    @pl.when(j == pl.num_programs(1) - 1)
    def _():
        o_ref[...] = x_ref[...] + gate_ref[0, 0] * _rms(acc_scr[...], gpost_ref[...])


def _ffn(x, g, mod, gpost, w_up, w_down, cw, cb, carry_in, *, layer, tm, rpt, tiles_per_group, tf):
    m, d = x.shape
    nj = D_FF // tf
    mrows = mod.shape[2]
    keep = (FFN_CONV_W - 1) * rpt
    pad = -(-keep // 8) * 8
    mod_spec = lambda k: pl.BlockSpec((1, 1, mrows, d), lambda i, j: (k, i // tiles_per_group, 0, 0))
    return pl.pallas_call(
        functools.partial(_ffn_kernel, tm=tm, rpt=rpt, tiles_per_group=tiles_per_group),
        grid=(m // tm, nj),
        in_specs=[pl.BlockSpec((tm, d), lambda i, j: (i, 0)),
                  pl.BlockSpec((1, d), lambda i, j: (0, 0)),
                  mod_spec(3), mod_spec(4), mod_spec(5),
                  pl.BlockSpec((1, d), lambda i, j: (0, 0)),
                  pl.BlockSpec((None, d, tf), lambda i, j: (layer, 0, j)),
                  pl.BlockSpec((None, d, tf), lambda i, j: (layer, 0, nj + j)),
                  pl.BlockSpec((None, tf, d), lambda i, j: (layer, j, 0)),
                  pl.BlockSpec((FFN_CONV_W, tf), lambda i, j: (0, j)),
                  pl.BlockSpec((1, tf), lambda i, j: (0, j)),
                  pl.BlockSpec((1, keep, tf), lambda i, j: (i // tiles_per_group, 0, j))],
        out_specs=[pl.BlockSpec((tm, d), lambda i, j: (i, 0)),
                   pl.BlockSpec((1, keep, tf), lambda i, j: (i, 0, j))],
        out_shape=[jax.ShapeDtypeStruct((m, d), F32),
                   jax.ShapeDtypeStruct((m // tm, keep, D_FF), F32)],
        scratch_shapes=[pltpu.VMEM((tm, d), BF16), pltpu.VMEM((tm, d), F32),
                        pltpu.VMEM((pad + tm, tf), F32), pltpu.VMEM((nj, keep, tf), F32)],
        compiler_params=_cparams(("arbitrary", "arbitrary")),
        name="ffn",
    )(x, g, mod, mod, mod, gpost, w_up, w_up, w_down, cw, cb, carry_in)


def _block_diag(w):
    nblk, bw, _ = w.shape
    eye = jnp.eye(nblk, dtype=w.dtype)
    return (eye[:, None, :, None] * w[:, :, None, :]).reshape(nblk * bw, nblk * bw)


def _prep_layer_weights(p):
    row = lambda a: a.reshape(1, -1)
    pad_lanes = lambda a: jnp.pad(a.reshape(1, -1), ((0, 0), (0, 128 - a.shape[-1])))
    return {
        'g_pre_mix': row(p['g_pre_mix']), 'g_post_mix': row(p['g_post_mix']),
        'g_pre_ffn': row(p['g_pre_ffn']), 'g_post_ffn': row(p['g_post_ffn']),
        'conv_lru_w': p['conv_lru_w'], 'conv_lru_b': row(p['conv_lru_b']),
        'w_lru_r': _block_diag(p['w_lru_r']).astype(BF16), 'b_lru_r': row(p['b_lru_r']),
        'w_lru_i': _block_diag(p['w_lru_i']).astype(BF16), 'b_lru_i': row(p['b_lru_i']),
        'lru_lambda': row(p['lru_lambda']), 'g_grp_lru': row(p['g_grp_lru']), 'g_grp_sb': row(p['g_grp_sb']),
        'sb_bias': p['sb_bias'],
        'conv_gdn_w': p['conv_gdn_w'],
        'gdn_a_log': pad_lanes(p['gdn_a_log']), 'gdn_dt_bias': pad_lanes(p['gdn_dt_bias']),
        'g_gdn_norm': row(p['g_gdn_norm']),
        'conv_ffn_w': p['conv_ffn_w'], 'conv_ffn_b': row(p['conv_ffn_b']),
    }


def _layer_prompt(x, mod, w, *, nb, t_len):
    tm = ROW_TILE
    tpg = t_len // tm
    tm_in = IN_PROJ_ROW_TILE
    proj, tail = _in_proj(x, w['g_pre_mix'], mod, w['w_in'], w['w_tail'], layer=w['layer'], tm=tm_in,
                          tiles_per_group=t_len // tm_in)
    zeros = lambda *s: jnp.zeros(s, F32)
    ya, h_last = _lru(proj, zeros(nb, CONV_W - 1, D_LRU), zeros(nb, 1, D_LRU), w['conv_lru_w'], w['conv_lru_b'],
                      w['w_lru_r'], w['b_lru_r'], w['w_lru_i'], w['b_lru_i'], w['lru_lambda'], w['g_grp_lru'],
                      nb=nb, t_len=t_len, rows=SEQ_TILE)
    ob = _sb_prompt(proj, w['sb_bias'], nb=nb, t_len=t_len, tq=SEQ_TILE)
    yc, s_new = _gdn(proj, tail, GDN_COL0, w['conv_gdn_w'], zeros(nb, CONV_W - 1, 3 * D_GDN),
                     w['gdn_a_log'], w['gdn_dt_bias'], zeros(nb, GDN_HEADS, GDN_HEAD_DIM, GDN_HEAD_DIM),
                     w['g_gdn_norm'], nb=nb, t_len=t_len, rows=2 * SEQ_TILE, chunk=GDN_CHUNK, valid_rows=None)
    x1 = _out_proj(ya, ob, yc, x, mod, w['g_grp_sb'], w['g_post_mix'], w['w_out'], layer=w['layer'], tm=tm,
                   tiles_per_group=tpg)
    x2, ffn_conv = _ffn(x1, w['g_pre_ffn'], mod, w['g_post_ffn'], w['w_ffn_up'], w['w_ffn_down'],
                        w['conv_ffn_w'], w['conv_ffn_b'], zeros(nb, FFN_CONV_W - 1, D_FF),
                        layer=w['layer'], tm=tm, rpt=1, tiles_per_group=tpg, tf=COL_TILE)
    ffn_conv = ffn_conv[tpg - 1::tpg]
    p3 = proj.reshape(nb, t_len, D_MAIN)
    k_new = p3[:, :, 3 * D_SB:4 * D_SB].reshape(nb, t_len, SB_HEADS, SB_HEAD_DIM)
    v_new = p3[:, :, 4 * D_SB:5 * D_SB].reshape(nb, t_len, SB_HEADS, SB_HEAD_DIM)
    lru_conv = p3[:, t_len - (CONV_W - 1):, 0:D_LRU]
    gdn_conv = p3[:, t_len - (CONV_W - 1):, GDN_COL0 * 128:GDN_COL0 * 128 + 3 * D_GDN]
    return x2, (k_new, v_new, lru_conv, h_last.reshape(nb, D_LRU), gdn_conv, s_new, ffn_conv)


def _layer_sample(x, mod_bm, mod_tm, w, layer, cache_k, cache_v, page_table, lru_buf, lru_h, gdn_buf, gdn_s,
                  ffn_buf, *, nb, t_len):
    m = nb * t_len
    proj, tail = _in_proj(x, w['g_pre_mix'], mod_bm, w['w_in'], w['w_tail'], layer=layer, tm=m,
                          tiles_per_group=1)
    ya, h_last = _lru(proj, lru_buf, lru_h.reshape(nb, 1, D_LRU), w['conv_lru_w'], w['conv_lru_b'],
                      w['w_lru_r'], w['b_lru_r'], w['w_lru_i'], w['b_lru_i'], w['lru_lambda'], w['g_grp_lru'],
                      nb=nb, t_len=t_len, rows=t_len)
    p3 = proj.reshape(nb, t_len, D_MAIN)
    q = p3[:, :, 2 * D_SB:3 * D_SB]
    k_new = p3[:, :, 3 * D_SB:4 * D_SB]
    v_new = p3[:, :, 4 * D_SB:5 * D_SB]
    pad_rows = lambda a: jnp.pad(a, ((0, 0), (0, PAGE_SIZE - t_len), (0, 0)))
    bias_rows = jnp.broadcast_to(jnp.repeat(w['sb_bias'], t_len)[:, None], (SB_HEADS * t_len, PAGE_SIZE))
    as_page = lambda a: pad_rows(a).transpose(0, 2, 1)
    ob = _sb_sample(q, bias_rows, as_page(k_new), as_page(v_new), cache_k, cache_v, page_table, layer=layer)
    ob = ob.reshape(m, D_SB)
    yc, s_new = _gdn(proj, tail, GDN_COL0, w['conv_gdn_w'], gdn_buf, w['gdn_a_log'], w['gdn_dt_bias'], gdn_s,
                     w['g_gdn_norm'], nb=nb, t_len=t_len, rows=t_len, chunk=GDN_CHUNK, valid_rows=t_len)
    x1 = _out_proj(ya, ob, yc, x, mod_bm, w['g_grp_sb'], w['g_post_mix'], w['w_out'], layer=layer, tm=m,
                   tiles_per_group=1)
    to_tm = lambda a: a.reshape(nb, -1, a.shape[-1]).transpose(1, 0, 2).reshape(-1, a.shape[-1])
    to_bm = lambda a: a.reshape(-1, nb, a.shape[-1]).transpose(1, 0, 2)
    x2_tm, ffn_conv_tm = _ffn(to_tm(x1), w['g_pre_ffn'], mod_tm, w['g_post_ffn'], w['w_ffn_up'], w['w_ffn_down'],
                              w['conv_ffn_w'], w['conv_ffn_b'], to_tm(ffn_buf)[None],
                              layer=layer, tm=m, rpt=nb, tiles_per_group=1, tf=COL_TILE)
    x2 = to_bm(x2_tm).reshape(m, D_MODEL)
    ffn_conv = to_bm(ffn_conv_tm[0])
    lru_conv = p3[:, t_len - (CONV_W - 1):, 0:D_LRU]
    gdn_conv = p3[:, t_len - (CONV_W - 1):, GDN_COL0 * 128:GDN_COL0 * 128 + 3 * D_GDN]
    hd = (nb, t_len, SB_HEADS, SB_HEAD_DIM)
    return x2, (k_new.reshape(hd), v_new.reshape(hd), lru_conv, h_last.reshape(nb, D_LRU), gdn_conv, s_new, ffn_conv)


def kernel(x_prompt, x_sample, c_prompt, c_sample, cache_sb_k, cache_sb_v, page_table, state_lru_conv, state_lru_h, state_gdn_conv, state_gdn, state_ffn_conv, w_ada, b_ada, g_pre_mix, g_post_mix, g_pre_ffn, g_post_ffn, w_in, conv_lru_w, conv_lru_b, w_lru_r, b_lru_r, w_lru_i, b_lru_i, lru_lambda, g_grp_lru, g_grp_sb, sb_bias, conv_gdn_w, gdn_a_log, gdn_dt_bias, g_gdn_norm, w_out, w_ffn_up, conv_ffn_w, conv_ffn_b, w_ffn_down):
    bp, t_p, d = x_prompt.shape
    bs, t_s, _ = x_sample.shape
    depth = w_ada.shape[0]
    n_pool = cache_sb_k.shape[1]
    assert bp + bs <= MOD_ROWS and d == D_MODEL

    c_all = jnp.concatenate([c_prompt, c_sample, jnp.zeros((MOD_ROWS - bp - bs, d), F32)], axis=0)
    mod_all = _ada(c_all, w_ada, b_ada).reshape(depth, MOD_ROWS, N_MOD, d)
    as_pages = lambda c: c.transpose(0, 1, 3, 4, 2).reshape(depth, n_pool, D_SB, PAGE_SIZE)
    cache_k = as_pages(cache_sb_k)
    cache_v = as_pages(cache_sb_v)

    params = dict(g_pre_mix=g_pre_mix, g_post_mix=g_post_mix, g_pre_ffn=g_pre_ffn, g_post_ffn=g_post_ffn,
                  conv_lru_w=conv_lru_w, conv_lru_b=conv_lru_b, w_lru_r=w_lru_r, b_lru_r=b_lru_r, w_lru_i=w_lru_i,
                  b_lru_i=b_lru_i, lru_lambda=lru_lambda, g_grp_lru=g_grp_lru, g_grp_sb=g_grp_sb, sb_bias=sb_bias,
                  conv_gdn_w=conv_gdn_w, gdn_a_log=gdn_a_log, gdn_dt_bias=gdn_dt_bias, g_gdn_norm=g_gdn_norm,
                  conv_ffn_w=conv_ffn_w, conv_ffn_b=conv_ffn_b)
    big = dict(w_in=w_in[:, :, :D_MAIN].astype(BF16),
               w_tail=jnp.pad(w_in[:, :, D_MAIN:], ((0, 0), (0, 0), (0, 128 - D_TAIL))).astype(BF16),
               w_out=w_out.astype(BF16), w_ffn_up=w_ffn_up.astype(BF16), w_ffn_down=w_ffn_down.astype(BF16))

    xp = x_prompt.reshape(bp * t_p, d)
    xs = x_sample.reshape(bs * t_s, d)
    outs_p, outs_s = [], []
    for l in range(depth):
        w = _prep_layer_weights({k: v[l] for k, v in params.items()})
        w.update(big, layer=l)
        mod_l = mod_all[l]
        mod_p = mod_l[:bp].transpose(1, 0, 2)[:, :, None, :]
        mod_s = mod_l[bp:bp + bs].transpose(1, 0, 2)
        mod_s_bm = jnp.repeat(mod_s, t_s, axis=1)[:, None]
        mod_s_tm = jnp.tile(mod_s, (1, t_s, 1))[:, None]
        xp, op = _layer_prompt(xp, mod_p, w, nb=bp, t_len=t_p)
        xs, os_ = _layer_sample(xs, mod_s_bm, mod_s_tm, w, l, cache_k, cache_v, page_table,
                                state_lru_conv[l], state_lru_h[l], state_gdn_conv[l], state_gdn[l],
                                state_ffn_conv[l], nb=bs, t_len=t_s)
        outs_p.append(op)
        outs_s.append(os_)

    stack = lambda outs, idx: jnp.stack([o[idx] for o in outs])
    res = [xp.reshape(bp, t_p, d), xs.reshape(bs, t_s, d),
           stack(outs_p, 0), stack(outs_p, 1), stack(outs_s, 0), stack(outs_s, 1)]
    for idx in range(2, 7):
        res += [stack(outs_p, idx), stack(outs_s, idx)]
    return tuple(res)
```

```python
import functools
import math

import jax
import jax.numpy as jnp
from jax import lax
from jax.experimental import pallas as pl
from jax.experimental.pallas import tpu as pltpu

F32 = jnp.float32
BF16 = jnp.bfloat16

D_MODEL = 2048
D_LRU = 512
LRU_BLOCKS = 8
LRU_C = 8.0
CONV_W = 4
D_SB = 512
SB_HEADS = 8
SB_HEAD_DIM = 64
D_GDN = 1024
GDN_HEADS = 8
GDN_HEAD_DIM = 128
D_FF = 5632
FFN_CONV_W = 3
N_MOD = 6
EPS = 1e-6
PAGE_SIZE = 128
D_MAIN = 2 * D_LRU + 3 * D_SB + 4 * D_GDN
D_TAIL = 2 * GDN_HEADS
GDN_COL0 = (2 * D_LRU + 3 * D_SB) // 128

VMEM_LIMIT = 56 * 1024 * 1024
MOD_ROWS = 16
SB_PAGES_PER_STEP = 16
SB_KEY_BLOCK = 256
SB_LOCKSTEP = 8
GDN_HEADS_PER_STEP = 4
GDN_CHUNK = 128
ROW_TILE = 512
IN_PROJ_ROW_TILE = 1024
COL_TILE = 512
SEQ_TILE = 256


def _cparams(sem):
    return pltpu.CompilerParams(dimension_semantics=sem, vmem_limit_bytes=VMEM_LIMIT)


def _dot(a, b):
    return jnp.dot(a, b, preferred_element_type=F32)


def _dot_nt(a, b):
    return lax.dot_general(a, b, (((1,), (1,)), ((), ())), preferred_element_type=F32)


def _split(a):
    hi = a.astype(BF16)
    lo = (a - hi.astype(F32)).astype(BF16)
    return hi, lo


def _mm3s(a, b):
    return _dot(a[0], b[0]) + (_dot(a[0], b[1]) + _dot(a[1], b[0]))


def _mm_exact_lhs(a_bf16, b):
    b_hi, b_lo = _split(b)
    return _dot(a_bf16, b_hi) + _dot(a_bf16, b_lo)


LOG2E = 1.4426950408889634


def _softplus(z):
    return jnp.maximum(z, 0.0) + jnp.log(1.0 + jnp.exp2(jnp.abs(z) * (-LOG2E)))


def _sigmoid(z):
    return jax.nn.sigmoid(z)


def _rms(x, g):
    return x * lax.rsqrt(jnp.mean(x * x, axis=-1, keepdims=True) + EPS) * g


def _ada_kernel(c_ref, w_ref, b_ref, o_ref):
    c = c_ref[...]
    s = c * _sigmoid(c)
    o_ref[0] = _dot(s.astype(BF16), w_ref[0].astype(BF16)) + b_ref[0]


def _ada(c_all, w_ada, b_ada):
    depth, d, n = w_ada.shape
    tn = 1024
    return pl.pallas_call(
        _ada_kernel,
        grid=(depth, n // tn),
        in_specs=[pl.BlockSpec((MOD_ROWS, d), lambda l, j: (0, 0)),
                  pl.BlockSpec((1, d, tn), lambda l, j: (l, 0, j)),
                  pl.BlockSpec((1, 1, tn), lambda l, j: (l, 0, j))],
        out_specs=pl.BlockSpec((1, MOD_ROWS, tn), lambda l, j: (l, 0, j)),
        out_shape=jax.ShapeDtypeStruct((depth, MOD_ROWS, n), F32),
        compiler_params=_cparams(("arbitrary", "arbitrary")),
        name="ada",
    )(c_all, w_ada, b_ada.reshape(depth, 1, n))


def _in_proj_kernel(x_ref, g_ref, sh_ref, sc_ref, w_ref, wt_ref, o_ref, ot_ref, h_scr):
    j = pl.program_id(1)

    @pl.when(j == 0)
    def _():
        h = _rms(x_ref[...], g_ref[...]) * (1.0 + sc_ref[0, 0]) + sh_ref[0, 0]
        hb = h.astype(BF16)
        h_scr[...] = hb
        ot_ref[...] = _dot_nt(hb, wt_ref[...])

    o_ref[...] = _dot_nt(h_scr[...], w_ref[...])


def _in_proj(x, g, mod, w_all, w_tail, *, layer, tm, tiles_per_group):
    m, d = x.shape
    n = D_MAIN
    tn = COL_TILE
    mrows = mod.shape[2]
    mod_spec = lambda k: pl.BlockSpec((1, 1, mrows, d), lambda i, j: (k, i // tiles_per_group, 0, 0))
    return pl.pallas_call(
        _in_proj_kernel,
        grid=(m // tm, n // tn),
        in_specs=[pl.BlockSpec((tm, d), lambda i, j: (i, 0)),
                  pl.BlockSpec((1, d), lambda i, j: (0, 0)),
                  mod_spec(0), mod_spec(1),
                  pl.BlockSpec((None, tn, d), lambda i, j: (layer, j, 0)),
                  pl.BlockSpec((None, 128, d), lambda i, j: (layer, 0, 0))],
        out_specs=[pl.BlockSpec((tm, tn), lambda i, j: (i, j)),
                   pl.BlockSpec((tm, 128), lambda i, j: (i, 0))],
        out_shape=[jax.ShapeDtypeStruct((m, n), F32),
                   jax.ShapeDtypeStruct((m, 128), F32)],
        scratch_shapes=[pltpu.VMEM((tm, d), BF16)],
        compiler_params=_cparams(("arbitrary", "arbitrary")),
        name="in_proj",
    )(x, g, mod, mod, w_all, w_tail)


def _lru_kernel(ax_ref, ag_ref, buf_ref, h0_ref, cw_ref, cb_ref, wr_ref, br_ref, wi_ref, bi_ref, lam_ref, gg_ref,
                y_ref, hl_ref, xp_scr, h_scr, *, rows):
    t = pl.program_id(1)

    @pl.when(t == 0)
    def _():
        xp_scr[5:8, :] = buf_ref[0]
        h_scr[...] = h0_ref[0]

    x = ax_ref[...]
    xp_scr[8:8 + rows, :] = x
    cw = cw_ref[...]
    xc = (xp_scr[5:5 + rows, :] * cw[0:1] + xp_scr[6:6 + rows, :] * cw[1:2]
          + xp_scr[7:7 + rows, :] * cw[2:3] + x * cw[3:4]) + cb_ref[...]
    xp_scr[5:8, :] = xp_scr[5 + rows:8 + rows, :]

    xb = xc.astype(BF16)
    r = _sigmoid(_dot(xb, wr_ref[...]) + br_ref[...])
    ig = _sigmoid(_dot(xb, wi_ref[...]) + bi_ref[...])
    log_a = (LRU_C * r) * (-_softplus(-lam_ref[...]))
    a = jnp.exp(log_a)
    u = jnp.sqrt(1.0 - a * a) * (ig * xc)

    ridx = lax.broadcasted_iota(jnp.int32, a.shape, 0)
    s = 1
    while s < rows:
        a_sh = pltpu.roll(a, s, 0)
        u_sh = pltpu.roll(u, s, 0)
        m = ridx >= s
        u = jnp.where(m, a * u_sh + u, u)
        a = jnp.where(m, a * a_sh, a)
        s *= 2
    h = a * h_scr[...] + u
    h_last = h[rows - 1:rows, :]
    h_scr[...] = h_last
    hl_ref[0] = h_last

    ag = ag_ref[...]
    gelu = 0.5 * ag * (1.0 + jnp.tanh(math.sqrt(2.0 / math.pi) * (ag + 0.044715 * (ag * ag * ag))))
    y_ref[...] = _rms(h * gelu, gg_ref[...])


def _lru(proj, buf, h0, cw, cb, wr, br, wi, bi, lam, gg, *, nb, t_len, rows):
    m = proj.shape[0]
    nt = t_len // rows
    row_map = lambda c: (lambda b, t: (b * nt + t, c))
    vec = lambda: pl.BlockSpec((1, D_LRU), lambda b, t: (0, 0))
    return pl.pallas_call(
        functools.partial(_lru_kernel, rows=rows),
        grid=(nb, nt),
        in_specs=[pl.BlockSpec((rows, D_LRU), row_map(0)),
                  pl.BlockSpec((rows, D_LRU), row_map(1)),
                  pl.BlockSpec((1, CONV_W - 1, D_LRU), lambda b, t: (b, 0, 0)),
                  pl.BlockSpec((1, 1, D_LRU), lambda b, t: (b, 0, 0)),
                  pl.BlockSpec((CONV_W, D_LRU), lambda b, t: (0, 0)),
                  vec(),
                  pl.BlockSpec((D_LRU, D_LRU), lambda b, t: (0, 0)), vec(),
                  pl.BlockSpec((D_LRU, D_LRU), lambda b, t: (0, 0)), vec(),
                  vec(), vec()],
        out_specs=[pl.BlockSpec((rows, D_LRU), lambda b, t: (b * nt + t, 0)),
                   pl.BlockSpec((1, 1, D_LRU), lambda b, t: (b, 0, 0))],
        out_shape=[jax.ShapeDtypeStruct((m, D_LRU), F32),
                   jax.ShapeDtypeStruct((nb, 1, D_LRU), F32)],
        scratch_shapes=[pltpu.VMEM((rows + 8, D_LRU), F32), pltpu.VMEM((1, D_LRU), F32)],
        compiler_params=_cparams(("arbitrary", "arbitrary")),
        name="lru",
    )(proj, proj, buf, h0, cw, cb, wr, br, wi, bi, lam, gg)


def _neg_tri2(n):
    row = lax.broadcasted_iota(jnp.int32, (2 * n, n), 0) & (n - 1)
    col = lax.broadcasted_iota(jnp.int32, (2 * n, n), 1)
    return jnp.where(row >= col, -1.0, 0.0).astype(BF16)


def _sb_stages(zs, mask, ntri2, carries):
    ss = [_softplus(z) for z in zs]
    if mask is not None:
        ss = [jnp.where(mask, s, 0.0) for s in ss]
    parts = [_split(s) for s in ss]
    after = [_dot(jnp.concatenate([hi, lo], axis=1), ntri2) for hi, lo in parts]
    ws = [jnp.exp(z + (a + c)) for z, a, c in zip(zs, after, carries)]
    if mask is not None:
        ws = [jnp.where(mask, w, 0.0) for w in ws]
    return ws, [jnp.sum(s, axis=1, keepdims=True) for s in ss]


def _sb_prompt_kernel(bias_ref, q_ref, k_ref, v_ref, o_ref, car_scr, *, tq):
    qi = pl.program_id(1)
    scale = SB_HEAD_DIM ** -0.5
    ntri2 = _neg_tri2(tq)
    row = lax.broadcasted_iota(jnp.int32, (tq, tq), 0)
    col = lax.broadcasted_iota(jnp.int32, (tq, tq), 1)
    causal = col < row
    lo_half = lax.broadcasted_iota(jnp.int32, (tq, 128), 1) < SB_HEAD_DIM
    npair = SB_HEADS // 2

    qm = []
    for pr in range(npair):
        qp = q_ref[:, pr * 128:(pr + 1) * 128] * scale
        qm.append((jnp.where(lo_half, qp, 0.0).astype(BF16), jnp.where(lo_half, 0.0, qp).astype(BF16)))
    o_ref[...] = jnp.zeros_like(o_ref)
    car_scr[...] = jnp.zeros_like(car_scr)

    def blocks(kb, mask):
        start = pl.multiple_of(kb * tq, tq)
        for grp in range(SB_HEADS // SB_LOCKSTEP):
            hs = list(range(grp * SB_LOCKSTEP, (grp + 1) * SB_LOCKSTEP))
            kp, vm = {}, {}
            for pr in sorted({h // 2 for h in hs}):
                cols = slice(pr * 128, (pr + 1) * 128)
                kp[pr] = k_ref[pl.ds(start, tq), cols].astype(BF16)
                vp = v_ref[pl.ds(start, tq), cols]
                vm[pr] = (jnp.where(lo_half, vp, 0.0).astype(BF16), jnp.where(lo_half, 0.0, vp).astype(BF16))
            z = [_dot_nt(qm[h // 2][h % 2], kp[h // 2]) + bias_ref[h] for h in hs]
            car = [car_scr[h] for h in hs]
            w, tot = _sb_stages(z, mask, ntri2, car)
            for h, c, t_ in zip(hs, car, tot):
                car_scr[h] = c - t_
            o = [_dot(x.astype(BF16), vm[h // 2][h % 2]) for x, h in zip(w, hs)]
            c0 = (hs[0] // 2) * 128
            o_ref[:, c0:c0 + 64 * len(hs)] += jnp.concatenate(
                [o[i] + o[i + 1] for i in range(0, len(hs), 2)], axis=1)

    blocks(qi, causal)

    def body(it, c):
        blocks(qi - 1 - it, None)
        return c

    lax.fori_loop(0, qi, body, 0)


def _sb_prompt(proj, bias, *, nb, t_len, tq):
    m = proj.shape[0]
    nq = t_len // tq
    qcol, kcol, vcol = (2 * D_LRU) // D_SB, (2 * D_LRU) // D_SB + 1, (2 * D_LRU) // D_SB + 2
    return pl.pallas_call(
        functools.partial(_sb_prompt_kernel, tq=tq),
        grid=(nb, nq),
        in_specs=[pl.BlockSpec(memory_space=pltpu.SMEM),
                  pl.BlockSpec((tq, D_SB), lambda b, i: (b * nq + i, qcol)),
                  pl.BlockSpec((t_len, D_SB), lambda b, i: (b, kcol)),
                  pl.BlockSpec((t_len, D_SB), lambda b, i: (b, vcol))],
        out_specs=pl.BlockSpec((tq, D_SB), lambda b, i: (b * nq + i, 0)),
        out_shape=jax.ShapeDtypeStruct((m, D_SB), F32),
        scratch_shapes=[pltpu.VMEM((SB_HEADS, tq, 1), F32)],
        compiler_params=_cparams(("arbitrary", "arbitrary")),
        name="sb_prompt",
    )(bias, proj, proj, proj)


def _sb_sample_kernel(pt_ref, q_ref, bias_ref, kn_ref, vn_ref, *rest, t_new, group):
    del pt_ref
    kc_refs, vc_refs = rest[:group], rest[group:2 * group]
    o_ref, acc_scr, car_scr = rest[2 * group:]
    p = pl.program_id(1)
    rows = SB_HEADS * t_new
    scale = SB_HEAD_DIM ** -0.5
    q = q_ref[0] * scale
    q_rep = jnp.concatenate([q] * SB_HEADS, axis=0)
    rh = lax.broadcasted_iota(jnp.int32, (rows, D_SB), 0) >> int(math.log2(t_new))
    ch = lax.broadcasted_iota(jnp.int32, (rows, D_SB), 1) >> int(math.log2(SB_HEAD_DIM))
    own = rh == ch
    qbd = jnp.where(own, q_rep, 0.0).astype(BF16)
    bias = bias_ref[...]

    def keys(kt, vt, width, mask):
        n = kt.shape[1]
        nblk = n // width
        z = _dot(qbd, kt) + jnp.concatenate([bias] * (n // PAGE_SIZE), axis=1)
        s = _softplus(z)
        if mask is not None:
            s = jnp.where(mask, s, 0.0)
        sk = [s[:, k * width:(k + 1) * width] for k in range(nblk)]
        tot = [jnp.sum(x, axis=1, keepdims=True) for x in sk]
        hi, lo = _split(jnp.concatenate(sk, axis=0) if nblk > 1 else sk[0])
        aft = _dot(jnp.concatenate([hi, lo], axis=1), _neg_tri2(width))
        car = car_scr[...]
        ws = [None] * nblk
        for k in reversed(range(nblk)):
            ws[k] = jnp.exp(z[:, k * width:(k + 1) * width] + (aft[k * rows:(k + 1) * rows] + car))
            car = car - tot[k]
        w = jnp.concatenate(ws, axis=1) if nblk > 1 else ws[0]
        if mask is not None:
            w = jnp.where(mask, w, 0.0)
        car_scr[...] = car
        acc_scr[...] += _dot_nt(w.astype(BF16), vt)

    @pl.when(p == 0)
    def _():
        acc_scr[...] = jnp.zeros_like(acc_scr)
        car_scr[...] = jnp.zeros_like(car_scr)
        qpos = lax.broadcasted_iota(jnp.int32, (rows, PAGE_SIZE), 0) & (t_new - 1)
        kpos = lax.broadcasted_iota(jnp.int32, (rows, PAGE_SIZE), 1)
        keys(kn_ref[0].astype(BF16), vn_ref[0].astype(BF16), PAGE_SIZE, kpos < qpos)

    keys(jnp.concatenate([r[...].astype(BF16) for r in kc_refs], axis=1),
         jnp.concatenate([r[...].astype(BF16) for r in vc_refs], axis=1), SB_KEY_BLOCK, None)

    @pl.when(p == pl.num_programs(1) - 1)
    def _():
        acc = jnp.where(own, acc_scr[...], 0.0)
        out = acc[0:t_new]
        for h in range(1, SB_HEADS):
            out = out + acc[h * t_new:(h + 1) * t_new]
        o_ref[0] = out


def _sb_sample(q, bias_rows, k_new, v_new, cache_k, cache_v, page_table, *, layer):
    nb, t_new, _ = q.shape
    n_pages = page_table.shape[1]
    rows = SB_HEADS * t_new
    group = SB_PAGES_PER_STEP
    assert n_pages % group == 0

    def cache_spec(g):
        return pl.BlockSpec((None, None, D_SB, PAGE_SIZE),
                            lambda b, p, pt: (layer, pt[b, n_pages - group * (p + 1) + g], 0, 0))

    grid_spec = pltpu.PrefetchScalarGridSpec(
        num_scalar_prefetch=1,
        grid=(nb, n_pages // group),
        in_specs=[pl.BlockSpec((1, t_new, D_SB), lambda b, p, pt: (b, 0, 0)),
                  pl.BlockSpec((rows, PAGE_SIZE), lambda b, p, pt: (0, 0)),
                  pl.BlockSpec((1, D_SB, PAGE_SIZE), lambda b, p, pt: (b, 0, 0)),
                  pl.BlockSpec((1, D_SB, PAGE_SIZE), lambda b, p, pt: (b, 0, 0))]
                 + [cache_spec(g) for g in range(group)] * 2,
        out_specs=pl.BlockSpec((1, t_new, D_SB), lambda b, p, pt: (b, 0, 0)),
        scratch_shapes=[pltpu.VMEM((rows, D_SB), F32), pltpu.VMEM((rows, 1), F32)],
    )
    return pl.pallas_call(
        functools.partial(_sb_sample_kernel, t_new=t_new, group=group),
        grid_spec=grid_spec,
        out_shape=jax.ShapeDtypeStruct((nb, t_new, D_SB), F32),
        compiler_params=_cparams(("arbitrary", "arbitrary")),
        name="sb_sample",
    )(page_table, q, bias_rows, k_new, v_new, *([cache_k] * group), *([cache_v] * group))


def _unit_lower_inverses(lmats, c, support):
    ii = lax.broadcasted_iota(jnp.int32, (c, c), 0)
    jj = lax.broadcasted_iota(jnp.int32, (c, c), 1)
    eye = jnp.where(ii == jj, 1.0, 0.0)
    base = min(16, c)
    shift = int(math.log2(base))
    same_base = (ii >> shift) == (jj >> shift)
    pws = [jnp.where(same_base, l, 0.0) for l in lmats]
    xs = [eye - p for p in pws]
    ps = [_split(p) for p in pws]
    n = 2
    while n < min(base, support):
        ps = [_split(_mm3s(p, p)) for p in ps]
        xs = [x + _mm3s(_split(x), p) for x, p in zip(xs, ps)]
        n *= 2
    s = base
    while s < min(c, support):
        sh = int(math.log2(s))
        below = ((ii >> (sh + 1)) == (jj >> (sh + 1))) & ((ii >> sh) != (jj >> sh))
        xsp = [_split(x) for x in xs]
        ys = [_mm3s(x, _split(jnp.where(below, l, 0.0))) for x, l in zip(xsp, lmats)]
        xs = [x - _mm3s(_split(y), xp) for x, y, xp in zip(xs, ys, xsp)]
        s *= 2
    return xs


def _gdn_kernel(q_ref, k_ref, v_ref, z_ref, tail_ref, cwq_ref, cwk_ref, cwv_ref, bq_ref, bk_ref, bv_ref,
                alog_ref, dtb_ref, s0_ref, gn_ref, y_ref, s_ref, xq_scr, xk_scr, xv_scr,
                *, rows, chunk, valid_rows, hg):
    t = pl.program_id(2)
    dk = GDN_HEAD_DIM

    @pl.when(t == 0)
    def _():
        xq_scr[5:8, :] = bq_ref[0]
        xk_scr[5:8, :] = bk_ref[0]
        xv_scr[5:8, :] = bv_ref[0]
        s_ref[0] = s0_ref[0]

    def conv_silu(x_ref, scr, cw_ref):
        x = x_ref[...]
        scr[8:8 + rows, :] = x
        cw = cw_ref[...]
        y = (scr[5:5 + rows, :] * cw[0:1] + scr[6:6 + rows, :] * cw[1:2]
             + scr[7:7 + rows, :] * cw[2:3] + x * cw[3:4])
        scr[5:8, :] = scr[5 + rows:8 + rows, :]
        return y * _sigmoid(y)

    q_all = conv_silu(q_ref, xq_scr, cwq_ref)
    k_all = conv_silu(k_ref, xk_scr, cwk_ref)
    v_all = conv_silu(v_ref, xv_scr, cwv_ref)

    tail = tail_ref[...]
    crows = -(-rows // chunk) * chunk
    if crows > rows:
        assert valid_rows is not None and valid_rows <= rows
        pad0 = lambda a: jnp.concatenate([a, jnp.zeros((crows - rows, a.shape[1]), F32)], axis=0)
        q_all, k_all, v_all, tail = pad0(q_all), pad0(k_all), pad0(v_all), pad0(tail)
    lane = lax.broadcasted_iota(jnp.int32, tail.shape, 1)
    lane1 = lax.broadcasted_iota(jnp.int32, (1, 128), 1)
    ii = lax.broadcasted_iota(jnp.int32, (chunk, chunk), 0)
    jj = lax.broadcasted_iota(jnp.int32, (chunk, chunk), 1)
    incl = ii >= jj
    strict = ii > jj
    ltri = jnp.where(incl, 1.0, 0.0).astype(BF16)
    support = chunk
    if valid_rows is not None:
        ok = lax.broadcasted_iota(jnp.int32, (crows, 1), 0) + t * rows < valid_rows
        support = min(chunk, 1 << max(valid_rows - 1, 0).bit_length())

    nchunk = crows // chunk
    heads = list(range(hg))
    chains = [(hh, c) for hh in heads for c in range(nchunk)]
    qn, kn, beta, g = [], [], [], []
    for hh in heads:
        h = pl.program_id(1) * hg + hh
        cs = slice(hh * dk, (hh + 1) * dk)
        q, k = q_all[:, cs], k_all[:, cs]
        qn.append(q * lax.rsqrt(jnp.sum(q * q, axis=-1, keepdims=True) + EPS) * (dk ** -0.5))
        kn.append(k * lax.rsqrt(jnp.sum(k * k, axis=-1, keepdims=True) + EPS))
        b_col = jnp.sum(jnp.where(lane == h, tail, 0.0), axis=1, keepdims=True)
        a_col = jnp.sum(jnp.where(lane == GDN_HEADS + h, tail, 0.0), axis=1, keepdims=True)
        a_log = jnp.sum(jnp.where(lane1 == h, alog_ref[...], 0.0), axis=1, keepdims=True)
        dt_b = jnp.sum(jnp.where(lane1 == h, dtb_ref[...], 0.0), axis=1, keepdims=True)
        b_h = _sigmoid(b_col)
        g_h = -jnp.exp(a_log) * _softplus(a_col + dt_b)
        if valid_rows is not None:
            b_h = jnp.where(ok, b_h, 0.0)
            g_h = jnp.where(ok, g_h, 0.0)
        beta.append(b_h)
        g.append(g_h)

    def per_chain(per_head, lanes=False):
        out = []
        for hh, c in chains:
            a = per_head[hh]
            if lanes:
                a = a[:, hh * dk:(hh + 1) * dk]
            out.append(a[c * chunk:(c + 1) * chunk])
        return out

    g_c, beta_c, q_c, k_c = per_chain(g), per_chain(beta), per_chain(qn), per_chain(kn)
    v_c = per_chain([v_all] * hg, lanes=True)
    gcum = [_mm_exact_lhs(ltri, jnp.broadcast_to(x, (chunk, dk))) for x in g_c]
    gdiff = [_mm_exact_lhs(ltri, jnp.where(strict, jnp.broadcast_to(x, (chunk, chunk)), 0.0)) for x in g_c]
    decay = [jnp.where(incl, jnp.exp(x), 0.0) for x in gdiff]
    kb = [a * b for a, b in zip(k_c, beta_c)]
    k_bf = [a.astype(BF16) for a in k_c]
    lmat = [jnp.where(strict, _dot_nt(a.astype(BF16), kk) * d, 0.0) for a, kk, d in zip(kb, k_bf, decay)]
    qk = [(_dot_nt(a.astype(BF16), kk) * d).astype(BF16) for a, kk, d in zip(q_c, k_bf, decay)]
    tinv = [x.astype(BF16) for x in _unit_lower_inverses(lmat, chunk, support)]
    egc = [jnp.exp(x) for x in gcum]
    u = [_dot(ti, (a * b).astype(BF16)) for ti, a, b in zip(tinv, v_c, beta_c)]
    w = [_dot(ti, (a * e).astype(BF16)).astype(BF16) for ti, a, e in zip(tinv, kb, egc)]
    g_last = [x[chunk - 1:chunk, :] for x in gcum]
    qg = [(a * e).astype(BF16) for a, e in zip(q_c, egc)]
    kd_t = [(a * jnp.exp(gl - gc)).T.astype(BF16) for a, gl, gc in zip(k_c, g_last, gcum)]
    eg = [jnp.exp(gl) for gl in g_last]

    gn = gn_ref[...]
    s = [s_ref[0, hh] for hh in heads]
    outs = [[] for _ in heads]
    for c in range(nchunk):
        idx = [hh * nchunk + c for hh in heads]
        sb = [x.astype(BF16) for x in s]
        v_new = [u[i] - _dot(w[i], sb[hh]) for hh, i in zip(heads, idx)]
        o_state = [_dot(qg[i], sb[hh]) for hh, i in zip(heads, idx)]
        vb = [x.astype(BF16) for x in v_new]
        o = [o_state[hh] + _dot(qk[i], vb[hh]) for hh, i in zip(heads, idx)]
        s = [s[hh] * eg[i] + _dot(kd_t[i], vb[hh]) for hh, i in zip(heads, idx)]
        for hh in heads:
            outs[hh].append(_rms(o[hh], gn))
    ys = [jnp.concatenate(o, axis=0) if nchunk > 1 else o[0] for o in outs]
    z_all = z_ref[...]
    y_ref[...] = jnp.concatenate(ys, axis=1)[:rows] * (z_all * _sigmoid(z_all))
    s_ref[0] = jnp.stack(s, axis=0)


def _gdn(arr, tail, col0, cw, buf, a_log, dt_bias, s0, gn, *, nb, t_len, rows, chunk, valid_rows):
    m = arr.shape[0]
    nt = t_len // rows
    hd = GDN_HEAD_DIM
    nh = GDN_HEADS
    hg = GDN_HEADS_PER_STEP
    wd = hg * hd
    assert nh % hg == 0 and col0 % hg == 0
    col = lambda base: pl.BlockSpec((rows, wd), lambda b, h, t: (b * nt + t, (col0 + base) // hg + h))
    cwspec = lambda base: pl.BlockSpec((CONV_W, wd), lambda b, h, t: (0, base // hg + h))
    bufspec = lambda base: pl.BlockSpec((1, CONV_W - 1, wd), lambda b, h, t: (b, 0, base // hg + h))
    vec = lambda: pl.BlockSpec((1, 128), lambda b, h, t: (0, 0))
    sspec = lambda: pl.BlockSpec((1, hg, hd, hd), lambda b, h, t: (b, h, 0, 0))
    return pl.pallas_call(
        functools.partial(_gdn_kernel, rows=rows, chunk=chunk, valid_rows=valid_rows, hg=hg),
        grid=(nb, nh // hg, nt),
        in_specs=[col(0), col(nh), col(2 * nh), col(3 * nh),
                  pl.BlockSpec((rows, 128), lambda b, h, t: (b * nt + t, 0)),
                  cwspec(0), cwspec(nh), cwspec(2 * nh),
                  bufspec(0), bufspec(nh), bufspec(2 * nh),
                  vec(), vec(), sspec(), vec()],
        out_specs=[pl.BlockSpec((rows, wd), lambda b, h, t: (b * nt + t, h)), sspec()],
        out_shape=[jax.ShapeDtypeStruct((m, D_GDN), F32),
                   jax.ShapeDtypeStruct((nb, nh, hd, hd), F32)],
        scratch_shapes=[pltpu.VMEM((rows + 8, wd), F32)] * 3,
        compiler_params=_cparams(("arbitrary", "arbitrary", "arbitrary")),
        name="gdn",
    )(arr, arr, arr, arr, tail, cw, cw, cw, buf, buf, buf, a_log, dt_bias, s0, gn)


def _out_proj_kernel(ya_ref, ob_ref, yc_ref, x_ref, gate_ref, gsb_ref, gpost_ref, w_ref, o_ref):
    yb = _rms(ob_ref[...], gsb_ref[...])
    mix = (_dot(ya_ref[...].astype(BF16), w_ref[0:D_LRU, :])
           + _dot(yb.astype(BF16), w_ref[D_LRU:D_LRU + D_SB, :])
           + _dot(yc_ref[...].astype(BF16), w_ref[D_LRU + D_SB:, :]))
    o_ref[...] = x_ref[...] + gate_ref[0, 0] * _rms(mix, gpost_ref[...])


def _out_proj(ya, ob, yc, x, mod, gsb, gpost, w_out, *, layer, tm, tiles_per_group):
    m, d = x.shape
    mrows = mod.shape[2]
    return pl.pallas_call(
        _out_proj_kernel,
        grid=(m // tm,),
        in_specs=[pl.BlockSpec((tm, D_LRU), lambda i: (i, 0)),
                  pl.BlockSpec((tm, D_SB), lambda i: (i, 0)),
                  pl.BlockSpec((tm, D_GDN), lambda i: (i, 0)),
                  pl.BlockSpec((tm, d), lambda i: (i, 0)),
                  pl.BlockSpec((1, 1, mrows, d), lambda i: (2, i // tiles_per_group, 0, 0)),
                  pl.BlockSpec((1, D_SB), lambda i: (0, 0)),
                  pl.BlockSpec((1, d), lambda i: (0, 0)),
                  pl.BlockSpec((None, d, d), lambda i: (layer, 0, 0))],
        out_specs=pl.BlockSpec((tm, d), lambda i: (i, 0)),
        out_shape=jax.ShapeDtypeStruct((m, d), F32),
        compiler_params=_cparams(("arbitrary",)),
        name="out_proj",
    )(ya, ob, yc, x, mod, gsb, gpost, w_out)


def _ffn_kernel(x_ref, g_ref, sh_ref, sc_ref, gate_ref, gpost_ref, wg_ref, wv_ref, wd_ref, cw_ref, cb_ref, cin_ref,
                o_ref, cout_ref, h_scr, acc_scr, gbuf_scr, carry_scr, *, tm, rpt, tiles_per_group):
    i = pl.program_id(0)
    j = pl.program_id(1)
    pad = gbuf_scr.shape[0] - tm
    keep = (FFN_CONV_W - 1) * rpt

    @pl.when(j == 0)
    def _():
        h = _rms(x_ref[...], g_ref[...]) * (1.0 + sc_ref[0, 0]) + sh_ref[0, 0]
        h_scr[...] = h.astype(BF16)
        acc_scr[...] = jnp.zeros_like(acc_scr)

    first = (i % tiles_per_group) == 0

    @pl.when(first)
    def _():
        gbuf_scr[pad - keep:pad, :] = cin_ref[0]

    @pl.when(jnp.logical_not(first))
    def _():
        gbuf_scr[pad - keep:pad, :] = carry_scr[j]

    hb = h_scr[...]
    gate = _dot(hb, wg_ref[...])
    val = _dot(hb, wv_ref[...])
    gbuf_scr[pad:pad + tm, :] = gate
    cw = cw_ref[...]
    gc = (gbuf_scr[pad - 2 * rpt:pad - 2 * rpt + tm, :] * cw[0:1]
          + gbuf_scr[pad - rpt:pad - rpt + tm, :] * cw[1:2] + gate * cw[2:3]) + cb_ref[...]
    last = gbuf_scr[pad + tm - keep:pad + tm, :]
    carry_scr[j] = last
    cout_ref[0] = last
    f = (gc * _sigmoid(gc)) * val
    acc_scr[...] += _dot(f.astype(BF16), wd_ref[...])

    @pl.when(j == pl.num_programs(1) - 1)
    def _():
        o_ref[...] = x_ref[...] + gate_ref[0, 0] * _rms(acc_scr[...], gpost_ref[...])


def _ffn(x, g, mod, gpost, w_up, w_down, cw, cb, carry_in, *, layer, tm, rpt, tiles_per_group, tf):
    m, d = x.shape
    nj = D_FF // tf
    mrows = mod.shape[2]
    keep = (FFN_CONV_W - 1) * rpt
    pad = -(-keep // 8) * 8
    mod_spec = lambda k: pl.BlockSpec((1, 1, mrows, d), lambda i, j: (k, i // tiles_per_group, 0, 0))
    return pl.pallas_call(
        functools.partial(_ffn_kernel, tm=tm, rpt=rpt, tiles_per_group=tiles_per_group),
        grid=(m // tm, nj),
        in_specs=[pl.BlockSpec((tm, d), lambda i, j: (i, 0)),
                  pl.BlockSpec((1, d), lambda i, j: (0, 0)),
                  mod_spec(3), mod_spec(4), mod_spec(5),
                  pl.BlockSpec((1, d), lambda i, j: (0, 0)),
                  pl.BlockSpec((None, d, tf), lambda i, j: (layer, 0, j)),
                  pl.BlockSpec((None, d, tf), lambda i, j: (layer, 0, nj + j)),
                  pl.BlockSpec((None, tf, d), lambda i, j: (layer, j, 0)),
                  pl.BlockSpec((FFN_CONV_W, tf), lambda i, j: (0, j)),
                  pl.BlockSpec((1, tf), lambda i, j: (0, j)),
                  pl.BlockSpec((1, keep, tf), lambda i, j: (i // tiles_per_group, 0, j))],
        out_specs=[pl.BlockSpec((tm, d), lambda i, j: (i, 0)),
                   pl.BlockSpec((1, keep, tf), lambda i, j: (i, 0, j))],
        out_shape=[jax.ShapeDtypeStruct((m, d), F32),
                   jax.ShapeDtypeStruct((m // tm, keep, D_FF), F32)],
        scratch_shapes=[pltpu.VMEM((tm, d), BF16), pltpu.VMEM((tm, d), F32),
                        pltpu.VMEM((pad + tm, tf), F32), pltpu.VMEM((nj, keep, tf), F32)],
        compiler_params=_cparams(("arbitrary", "arbitrary")),
        name="ffn",
    )(x, g, mod, mod, mod, gpost, w_up, w_up, w_down, cw, cb, carry_in)


def _block_diag(w):
    nblk, bw, _ = w.shape
    eye = jnp.eye(nblk, dtype=w.dtype)
    return (eye[:, None, :, None] * w[:, :, None, :]).reshape(nblk * bw, nblk * bw)


def _prep_layer_weights(p):
    row = lambda a: a.reshape(1, -1)
    pad_lanes = lambda a: jnp.pad(a.reshape(1, -1), ((0, 0), (0, 128 - a.shape[-1])))
    w_in = p['w_in']
    return {
        'w_in': w_in.T.astype(BF16)[None],
        'w_tail': jnp.pad(w_in[:, D_MAIN:].T, ((0, 128 - D_TAIL), (0, 0))).astype(BF16)[None],
        'g_pre_mix': row(p['g_pre_mix']), 'g_post_mix': row(p['g_post_mix']),
        'g_pre_ffn': row(p['g_pre_ffn']), 'g_post_ffn': row(p['g_post_ffn']),
        'conv_lru_w': p['conv_lru_w'], 'conv_lru_b': row(p['conv_lru_b']),
        'w_lru_r': _block_diag(p['w_lru_r']).astype(BF16), 'b_lru_r': row(p['b_lru_r']),
        'w_lru_i': _block_diag(p['w_lru_i']).astype(BF16), 'b_lru_i': row(p['b_lru_i']),
        'lru_lambda': row(p['lru_lambda']), 'g_grp_lru': row(p['g_grp_lru']), 'g_grp_sb': row(p['g_grp_sb']),
        'sb_bias': p['sb_bias'],
        'conv_gdn_w': p['conv_gdn_w'],
        'gdn_a_log': pad_lanes(p['gdn_a_log']), 'gdn_dt_bias': pad_lanes(p['gdn_dt_bias']),
        'g_gdn_norm': row(p['g_gdn_norm']),
        'conv_ffn_w': p['conv_ffn_w'], 'conv_ffn_b': row(p['conv_ffn_b']),
    }


def _layer_prompt(x, mod, w, *, nb, t_len):
    tm = ROW_TILE
    tpg = t_len // tm
    tm_in = IN_PROJ_ROW_TILE
    proj, tail = _in_proj(x, w['g_pre_mix'], mod, w['w_in'], w['w_tail'], layer=0, tm=tm_in,
                          tiles_per_group=t_len // tm_in)
    zeros = lambda *s: jnp.zeros(s, F32)
    ya, h_last = _lru(proj, zeros(nb, CONV_W - 1, D_LRU), zeros(nb, 1, D_LRU), w['conv_lru_w'], w['conv_lru_b'],
                      w['w_lru_r'], w['b_lru_r'], w['w_lru_i'], w['b_lru_i'], w['lru_lambda'], w['g_grp_lru'],
                      nb=nb, t_len=t_len, rows=SEQ_TILE)
    ob = _sb_prompt(proj, w['sb_bias'], nb=nb, t_len=t_len, tq=SEQ_TILE)
    yc, s_new = _gdn(proj, tail, GDN_COL0, w['conv_gdn_w'], zeros(nb, CONV_W - 1, 3 * D_GDN),
                     w['gdn_a_log'], w['gdn_dt_bias'], zeros(nb, GDN_HEADS, GDN_HEAD_DIM, GDN_HEAD_DIM),
                     w['g_gdn_norm'], nb=nb, t_len=t_len, rows=2 * SEQ_TILE, chunk=GDN_CHUNK, valid_rows=None)
    x1 = _out_proj(ya, ob, yc, x, mod, w['g_grp_sb'], w['g_post_mix'], w['w_out'], layer=w['layer'], tm=tm,
                   tiles_per_group=tpg)
    x2, ffn_conv = _ffn(x1, w['g_pre_ffn'], mod, w['g_post_ffn'], w['w_ffn_up'], w['w_ffn_down'],
                        w['conv_ffn_w'], w['conv_ffn_b'], zeros(nb, FFN_CONV_W - 1, D_FF),
                        layer=w['layer'], tm=tm, rpt=1, tiles_per_group=tpg, tf=COL_TILE)
    ffn_conv = ffn_conv[tpg - 1::tpg]
    p3 = proj.reshape(nb, t_len, D_MAIN)
    k_new = p3[:, :, 3 * D_SB:4 * D_SB].reshape(nb, t_len, SB_HEADS, SB_HEAD_DIM)
    v_new = p3[:, :, 4 * D_SB:5 * D_SB].reshape(nb, t_len, SB_HEADS, SB_HEAD_DIM)
    lru_conv = p3[:, t_len - (CONV_W - 1):, 0:D_LRU]
    gdn_conv = p3[:, t_len - (CONV_W - 1):, GDN_COL0 * 128:GDN_COL0 * 128 + 3 * D_GDN]
    return x2, (k_new, v_new, lru_conv, h_last.reshape(nb, D_LRU), gdn_conv, s_new, ffn_conv)


def _layer_sample(x, mod_bm, mod_tm, w, layer, cache_k, cache_v, page_table, lru_buf, lru_h, gdn_buf, gdn_s,
                  ffn_buf, *, nb, t_len):
    m = nb * t_len
    proj, tail = _in_proj(x, w['g_pre_mix'], mod_bm, w['w_in'], w['w_tail'], layer=0, tm=m,
                          tiles_per_group=1)
    ya, h_last = _lru(proj, lru_buf, lru_h.reshape(nb, 1, D_LRU), w['conv_lru_w'], w['conv_lru_b'],
                      w['w_lru_r'], w['b_lru_r'], w['w_lru_i'], w['b_lru_i'], w['lru_lambda'], w['g_grp_lru'],
                      nb=nb, t_len=t_len, rows=t_len)
    p3 = proj.reshape(nb, t_len, D_MAIN)
    q = p3[:, :, 2 * D_SB:3 * D_SB]
    k_new = p3[:, :, 3 * D_SB:4 * D_SB]
    v_new = p3[:, :, 4 * D_SB:5 * D_SB]
    pad_rows = lambda a: jnp.pad(a, ((0, 0), (0, PAGE_SIZE - t_len), (0, 0)))
    bias_rows = jnp.broadcast_to(jnp.repeat(w['sb_bias'], t_len)[:, None], (SB_HEADS * t_len, PAGE_SIZE))
    as_page = lambda a: pad_rows(a).transpose(0, 2, 1)
    ob = _sb_sample(q, bias_rows, as_page(k_new), as_page(v_new), cache_k, cache_v, page_table, layer=layer)
    ob = ob.reshape(m, D_SB)
    yc, s_new = _gdn(proj, tail, GDN_COL0, w['conv_gdn_w'], gdn_buf, w['gdn_a_log'], w['gdn_dt_bias'], gdn_s,
                     w['g_gdn_norm'], nb=nb, t_len=t_len, rows=t_len, chunk=GDN_CHUNK, valid_rows=t_len)
    x1 = _out_proj(ya, ob, yc, x, mod_bm, w['g_grp_sb'], w['g_post_mix'], w['w_out'], layer=layer, tm=m,
                   tiles_per_group=1)
    to_tm = lambda a: a.reshape(nb, -1, a.shape[-1]).transpose(1, 0, 2).reshape(-1, a.shape[-1])
    to_bm = lambda a: a.reshape(-1, nb, a.shape[-1]).transpose(1, 0, 2)
    x2_tm, ffn_conv_tm = _ffn(to_tm(x1), w['g_pre_ffn'], mod_tm, w['g_post_ffn'], w['w_ffn_up'], w['w_ffn_down'],
                              w['conv_ffn_w'], w['conv_ffn_b'], to_tm(ffn_buf)[None],
                              layer=layer, tm=m, rpt=nb, tiles_per_group=1, tf=COL_TILE)
    x2 = to_bm(x2_tm).reshape(m, D_MODEL)
    ffn_conv = to_bm(ffn_conv_tm[0])
    lru_conv = p3[:, t_len - (CONV_W - 1):, 0:D_LRU]
    gdn_conv = p3[:, t_len - (CONV_W - 1):, GDN_COL0 * 128:GDN_COL0 * 128 + 3 * D_GDN]
    hd = (nb, t_len, SB_HEADS, SB_HEAD_DIM)
    return x2, (k_new.reshape(hd), v_new.reshape(hd), lru_conv, h_last.reshape(nb, D_LRU), gdn_conv, s_new, ffn_conv)


def kernel(x_prompt, x_sample, c_prompt, c_sample, cache_sb_k, cache_sb_v, page_table, state_lru_conv, state_lru_h, state_gdn_conv, state_gdn, state_ffn_conv, w_ada, b_ada, g_pre_mix, g_post_mix, g_pre_ffn, g_post_ffn, w_in, conv_lru_w, conv_lru_b, w_lru_r, b_lru_r, w_lru_i, b_lru_i, lru_lambda, g_grp_lru, g_grp_sb, sb_bias, conv_gdn_w, gdn_a_log, gdn_dt_bias, g_gdn_norm, w_out, w_ffn_up, conv_ffn_w, conv_ffn_b, w_ffn_down):
    bp, t_p, d = x_prompt.shape
    bs, t_s, _ = x_sample.shape
    depth = w_ada.shape[0]
    n_pool = cache_sb_k.shape[1]
    assert bp + bs <= MOD_ROWS and d == D_MODEL

    c_all = jnp.concatenate([c_prompt, c_sample, jnp.zeros((MOD_ROWS - bp - bs, d), F32)], axis=0)
    mod_all = _ada(c_all, w_ada, b_ada).reshape(depth, MOD_ROWS, N_MOD, d)
    as_pages = lambda c: c.transpose(0, 1, 3, 4, 2).reshape(depth, n_pool, D_SB, PAGE_SIZE)
    cache_k = as_pages(cache_sb_k)
    cache_v = as_pages(cache_sb_v)

    params = dict(w_in=w_in, g_pre_mix=g_pre_mix, g_post_mix=g_post_mix, g_pre_ffn=g_pre_ffn, g_post_ffn=g_post_ffn,
                  conv_lru_w=conv_lru_w, conv_lru_b=conv_lru_b, w_lru_r=w_lru_r, b_lru_r=b_lru_r, w_lru_i=w_lru_i,
                  b_lru_i=b_lru_i, lru_lambda=lru_lambda, g_grp_lru=g_grp_lru, g_grp_sb=g_grp_sb, sb_bias=sb_bias,
                  conv_gdn_w=conv_gdn_w, gdn_a_log=gdn_a_log, gdn_dt_bias=gdn_dt_bias, g_gdn_norm=g_gdn_norm,
                  conv_ffn_w=conv_ffn_w, conv_ffn_b=conv_ffn_b)
    big = dict(w_out=w_out.astype(BF16), w_ffn_up=w_ffn_up.astype(BF16), w_ffn_down=w_ffn_down.astype(BF16))

    xp = x_prompt.reshape(bp * t_p, d)
    xs = x_sample.reshape(bs * t_s, d)
    outs_p, outs_s = [], []
    for l in range(depth):
        w = _prep_layer_weights({k: v[l] for k, v in params.items()})
        w.update(big, layer=l)
        mod_l = mod_all[l]
        mod_p = mod_l[:bp].transpose(1, 0, 2)[:, :, None, :]
        mod_s = mod_l[bp:bp + bs].transpose(1, 0, 2)
        mod_s_bm = jnp.repeat(mod_s, t_s, axis=1)[:, None]
        mod_s_tm = jnp.tile(mod_s, (1, t_s, 1))[:, None]
        xp, op = _layer_prompt(xp, mod_p, w, nb=bp, t_len=t_p)
        xs, os_ = _layer_sample(xs, mod_s_bm, mod_s_tm, w, l, cache_k, cache_v, page_table,
                                state_lru_conv[l], state_lru_h[l], state_gdn_conv[l], state_gdn[l],
                                state_ffn_conv[l], nb=bs, t_len=t_s)
        outs_p.append(op)
        outs_s.append(os_)

    stack = lambda outs, idx: jnp.stack([o[idx] for o in outs])
    res = [xp.reshape(bp, t_p, d), xs.reshape(bs, t_s, d),
           stack(outs_p, 0), stack(outs_p, 1), stack(outs_s, 0), stack(outs_s, 1)]
    for idx in range(2, 7):
        res += [stack(outs_p, idx), stack(outs_s, idx)]
    return tuple(res)
```
